```python
import jax, jax.numpy as jnp
from jax import lax
import numpy as np

D_MODEL = 1024
BATCH = 2
SEQ = 8192
DEPTH = 2
DEC_BATCH = 128
DEC_SEQ = 1
PAST_LEN = 2048
PAGE_SIZE = 128

HEAD_DIM = 64
N_A_LAYERS = DEPTH // 2
N_B_LAYERS = DEPTH - N_A_LAYERS
FOX_HEADS = D_MODEL // HEAD_DIM
DIL_WINDOWS = (128, 512, 2048)
DIL_RATES = (1, 4, 16)
N_DIL_GROUPS = len(DIL_WINDOWS)
DIL_HEADS = D_MODEL // (2 * HEAD_DIM)
N_EXPERT_GROUPS = 4
EXPERTS_PER_GROUP = 8
N_EXPERTS = N_EXPERT_GROUPS * EXPERTS_PER_GROUP
TOP_K_IN_GROUP = 2
EXPERT_FF = D_MODEL // 4
Q_BLOCK = 128
LN_EPS = 1e-5
FORGET_BIAS_INIT = 3.0
ATTN_SCALE = HEAD_DIM ** -0.5
DEEPNORM_ALPHA = (2 * DEPTH) ** 0.25
DEEPNORM_BETA = (8 * DEPTH) ** -0.25

kernel_name = "yoco_fox_dilated_hmoe_step"


def _layer_norm(x, g, b):
    xf = x.astype(jnp.float32)
    xc = xf - jnp.mean(xf, axis=-1, keepdims=True)
    var = jnp.mean(xc * xc, axis=-1, keepdims=True)
    return (xc * lax.rsqrt(var + LN_EPS) * g.astype(jnp.float32) + b.astype(jnp.float32)).astype(x.dtype)


def _post_norm(x, sub, g, b):
    return _layer_norm(DEEPNORM_ALPHA * x + sub, g, b)


def _query_blocks(fn, *q_arrays):
    t = q_arrays[0].shape[1]
    if t <= Q_BLOCK or t % Q_BLOCK:
        return fn(0, *q_arrays)
    nb = t // Q_BLOCK
    blocked = tuple(a.reshape(a.shape[0], nb, Q_BLOCK, *a.shape[2:]).swapaxes(0, 1) for a in q_arrays)
    outs = lax.map(lambda args: fn(args[0] * Q_BLOCK, *args[1:]), (jnp.arange(nb), *blocked))
    return tuple(o.swapaxes(0, 1).reshape(o.shape[1], t, *o.shape[3:]) for o in outs)


def _fox_project(x, w, b_f):
    b, t, _ = x.shape
    y = x @ w
    qkv = y[..., :3 * FOX_HEADS * HEAD_DIM].reshape(b, t, 3, FOX_HEADS, HEAD_DIM)
    logf = jax.nn.log_sigmoid((y[..., 3 * FOX_HEADS * HEAD_DIM:] + b_f).astype(jnp.float32))
    return qkv[:, :, 0], qkv[:, :, 1], qkv[:, :, 2], logf


def _fox_attend(q, k_ctx, v_ctx, logf_ctx, q_start):
    t = q.shape[1]
    length = k_ctx.shape[1]
    cum = jnp.cumsum(logf_ctx.astype(jnp.float32), axis=1)
    cum_k = cum.transpose(0, 2, 1)
    cum_q = cum[:, q_start:q_start + t]
    k_pos = jnp.arange(length)

    def block(start, qb, cqb):
        q_pos = q_start + start + jnp.arange(qb.shape[1])
        s = jnp.einsum('bqhd,bkhd->bhqk', qb, k_ctx, preferred_element_type=jnp.float32) * ATTN_SCALE
        s = s + cqb.transpose(0, 2, 1)[..., None] - cum_k[:, :, None, :]
        s = jnp.where(k_pos[None, :] <= q_pos[:, None], s, -jnp.inf)
        p = jax.nn.softmax(s, axis=-1)
        return (jnp.einsum('bhqk,bkhd->bqhd', p.astype(v_ctx.dtype), v_ctx),)

    (o,) = _query_blocks(block, q, cum_q)
    return o


def _alibi_slopes():
    n = N_DIL_GROUPS * DIL_HEADS
    idx = jnp.arange(DIL_HEADS)[None, :] * N_DIL_GROUPS + jnp.arange(N_DIL_GROUPS)[:, None] + 1
    return jnp.exp2(-8.0 * idx.astype(jnp.float32) / n)


def _dilated_attend(q, kv_ctx, q_start, window, rate, slopes):
    n_keys = window // rate + 1
    dist = rate * jnp.arange(n_keys)
    alibi = -slopes[:, None] * dist.astype(jnp.float32)[None, :]

    def block(start, qb):
        idx = q_start + start + jnp.arange(qb.shape[1])[:, None] - dist[None, :]
        valid = idx >= 0
        kvg = kv_ctx[:, jnp.maximum(idx, 0)]
        s = jnp.einsum('bqhd,bqnhd->bhqn', qb, kvg[:, :, :, 0], preferred_element_type=jnp.float32) * ATTN_SCALE
        s = jnp.where(valid[None, None], s + alibi[None, :, None, :], -jnp.inf)
        lse = jax.nn.logsumexp(s, axis=-1, keepdims=True)
        p = jnp.exp(s - lse)
        o = jnp.einsum('bhqn,bqnhd->bqhd', p.astype(kv_ctx.dtype), kvg[:, :, :, 1])
        return o, lse[..., 0].transpose(0, 2, 1)

    return _query_blocks(block, q)


def _shared_kv(x, w_kv):
    b, t, _ = x.shape
    kv = (x @ w_kv).reshape(b, t, N_DIL_GROUPS, 2, DIL_HEADS, HEAD_DIM)
    return [kv[:, :, g] for g in range(N_DIL_GROUPS)]


def _dilated_mixer(x, w_q, kv_ctx, q_starts, w_o):
    b, t, _ = x.shape
    q = (x @ w_q).reshape(b, t, N_DIL_GROUPS, DIL_HEADS, HEAD_DIM)
    slopes = _alibi_slopes()
    outs, lses = [], []
    for g in range(N_DIL_GROUPS):
        o, lse = _dilated_attend(q[:, :, g], kv_ctx[g], q_starts[g], DIL_WINDOWS[g], DIL_RATES[g], slopes[g])
        outs.append(o)
        lses.append(lse)
    wts = jax.nn.softmax(jnp.stack(lses), axis=0)
    o = jnp.einsum('gbth,gbthd->bthd', wts.astype(outs[0].dtype), jnp.stack(outs))
    return o.reshape(b, t, DIL_HEADS * HEAD_DIM) @ w_o


def _hier_moe(x, w_rg, b_rg, w_re, b_re, w_gate, w_up, w_down):
    lead = x.shape[:-1]
    xt = x.reshape(-1, D_MODEL)
    n = xt.shape[0]
    grp_logits = (xt @ w_rg + b_rg).astype(jnp.float32)
    g_sel = jnp.argmax(grp_logits, axis=-1)
    p_sel = jnp.take_along_axis(jax.nn.softmax(grp_logits, axis=-1), g_sel[:, None], axis=1)
    exp_logits = (xt @ w_re + b_re).astype(jnp.float32).reshape(n, N_EXPERT_GROUPS, EXPERTS_PER_GROUP)
    in_logits = jnp.take_along_axis(exp_logits, g_sel[:, None, None], axis=1)[:, 0]
    top_v, top_i = lax.top_k(in_logits, TOP_K_IN_GROUP)
    w_sel = jax.nn.softmax(top_v, axis=-1) * p_sel
    eid = g_sel[:, None] * EXPERTS_PER_GROUP + top_i
    gates = jnp.einsum('nk,nke->ne', w_sel, jax.nn.one_hot(eid, N_EXPERTS, dtype=jnp.float32))
    h = jax.nn.silu(jnp.einsum('nd,edf->nef', xt, w_gate)) * jnp.einsum('nd,edf->nef', xt, w_up)
    y = jnp.einsum('nef,efd->nd', h * gates[..., None].astype(h.dtype), w_down)
    return y.reshape(*lead, D_MODEL)


def _trunk(x, fox_context, dil_context, w_qkvf_a, b_f_a, w_o_a, w_q_b, w_kv_b, w_o_b, ln_g, ln_b,
           w_router_grp, b_router_grp, w_router_exp, b_router_exp, w_gate, w_up, w_down):
    b, t, _ = x.shape
    fox_rows = []
    dil_ctx, q_starts, dil_state = None, None, None
    for l in range(DEPTH):
        if l < N_A_LAYERS:
            q, k, v, logf = _fox_project(x, w_qkvf_a[l], b_f_a[l])
            k_ctx, v_ctx, lf_ctx, q_start = fox_context(l, k, v, logf)
            o = _fox_attend(q, k_ctx, v_ctx, lf_ctx, q_start)
            mix = o.reshape(b, t, FOX_HEADS * HEAD_DIM) @ w_o_a[l]
            fox_rows.append((k, v, logf))
        else:
            if l == N_A_LAYERS:
                dil_ctx, q_starts, dil_state = dil_context(_shared_kv(x, w_kv_b))
            lb = l - N_A_LAYERS
            mix = _dilated_mixer(x, w_q_b[lb], dil_ctx, q_starts, w_o_b[lb])
        x = _post_norm(x, mix, ln_g[l, 0], ln_b[l, 0])
        ffn = _hier_moe(x, w_router_grp[l], b_router_grp[l], w_router_exp[l], b_router_exp[l],
                        w_gate[l], w_up[l], w_down[l])
        x = _post_norm(x, ffn, ln_g[l, 1], ln_b[l, 1])
    fox_k = jnp.stack([r[0] for r in fox_rows])
    fox_v = jnp.stack([r[1] for r in fox_rows])
    fox_lf = jnp.stack([r[2] for r in fox_rows])
    return x, fox_k, fox_v, fox_lf, dil_state


def setup_inputs(seed: int = 0) -> dict:
    key = jax.random.key(seed)
    ks = jax.random.split(key, 32)
    f32 = jnp.float32

    def nrm(i, shape, scale=1.0):
        return scale * jax.random.normal(ks[i], shape, f32)

    n_pages = PAST_LEN // PAGE_SIZE
    n_used = DEC_BATCH * n_pages
    n_phys = n_used + max(1, n_used // 4)
    page_table = jax.random.permutation(ks[0], n_phys)[:n_used].astype(jnp.int32).reshape(DEC_BATCH, n_pages)
    d_qkvf = 3 * FOX_HEADS * HEAD_DIM + FOX_HEADS
    d_q_b = N_DIL_GROUPS * DIL_HEADS * HEAD_DIM
    d_kv_b = N_DIL_GROUPS * 2 * DIL_HEADS * HEAD_DIM
    win = [min(w, PAST_LEN) for w in DIL_WINDOWS]
    return {
        'x_prompt': nrm(1, (BATCH, SEQ, D_MODEL)),
        'x_sample': nrm(2, (DEC_BATCH, DEC_SEQ, D_MODEL)),
        'cache_fox_k': nrm(3, (N_A_LAYERS, n_phys, PAGE_SIZE, FOX_HEADS, HEAD_DIM)),
        'cache_fox_v': nrm(4, (N_A_LAYERS, n_phys, PAGE_SIZE, FOX_HEADS, HEAD_DIM)),
        'cache_fox_logf': jax.nn.log_sigmoid(FORGET_BIAS_INIT + nrm(5, (N_A_LAYERS, n_phys, PAGE_SIZE, FOX_HEADS))),
        'cache_dil_kv_0': nrm(6, (DEC_BATCH, win[0], 2, DIL_HEADS, HEAD_DIM)),
        'cache_dil_kv_1': nrm(7, (DEC_BATCH, win[1], 2, DIL_HEADS, HEAD_DIM)),
        'cache_dil_kv_2': nrm(8, (DEC_BATCH, win[2], 2, DIL_HEADS, HEAD_DIM)),
        'page_table': page_table,
        'w_qkvf_a': nrm(9, (N_A_LAYERS, D_MODEL, d_qkvf), D_MODEL ** -0.5),
        'b_f_a': FORGET_BIAS_INIT + nrm(10, (N_A_LAYERS, FOX_HEADS), 0.1),
        'w_o_a': nrm(11, (N_A_LAYERS, FOX_HEADS * HEAD_DIM, D_MODEL), DEEPNORM_BETA * (FOX_HEADS * HEAD_DIM) ** -0.5),
        'w_q_b': nrm(12, (N_B_LAYERS, D_MODEL, d_q_b), D_MODEL ** -0.5),
        'w_kv_b': nrm(13, (D_MODEL, d_kv_b), D_MODEL ** -0.5),
        'w_o_b': nrm(14, (N_B_LAYERS, DIL_HEADS * HEAD_DIM, D_MODEL), DEEPNORM_BETA * (DIL_HEADS * HEAD_DIM) ** -0.5),
        'ln_g': 1.0 + nrm(15, (DEPTH, 2, D_MODEL), 0.02),
        'ln_b': nrm(16, (DEPTH, 2, D_MODEL), 0.02),
        'w_router_grp': nrm(17, (DEPTH, D_MODEL, N_EXPERT_GROUPS), D_MODEL ** -0.5),
        'b_router_grp': nrm(18, (DEPTH, N_EXPERT_GROUPS), 0.01),
        'w_router_exp': nrm(19, (DEPTH, D_MODEL, N_EXPERTS), D_MODEL ** -0.5),
        'b_router_exp': nrm(20, (DEPTH, N_EXPERTS), 0.01),
        'w_gate': nrm(21, (DEPTH, N_EXPERTS, D_MODEL, EXPERT_FF), D_MODEL ** -0.5),
        'w_up': nrm(22, (DEPTH, N_EXPERTS, D_MODEL, EXPERT_FF), D_MODEL ** -0.5),
        'w_down': nrm(23, (DEPTH, N_EXPERTS, EXPERT_FF, D_MODEL), DEEPNORM_BETA * EXPERT_FF ** -0.5),
    }


def reference(x_prompt, x_sample, cache_fox_k, cache_fox_v, cache_fox_logf, cache_dil_kv_0, cache_dil_kv_1,
              cache_dil_kv_2, page_table, w_qkvf_a, b_f_a, w_o_a, w_q_b, w_kv_b, w_o_b, ln_g, ln_b,
              w_router_grp, b_router_grp, w_router_exp, b_router_exp, w_gate, w_up, w_down):
    weights = (w_qkvf_a, b_f_a, w_o_a, w_q_b, w_kv_b, w_o_b, ln_g, ln_b, w_router_grp, b_router_grp,
               w_router_exp, b_router_exp, w_gate, w_up, w_down)

    def prompt_fox(l, k, v, logf):
        return k, v, logf, 0

    def prompt_dil(kv):
        states = [g_kv[:, -min(w, g_kv.shape[1]):] for w, g_kv in zip(DIL_WINDOWS, kv)]
        return kv, [0] * N_DIL_GROUPS, states

    def sample_fox(l, k, v, logf):
        db = k.shape[0]
        k_past = cache_fox_k[l, page_table].reshape(db, -1, FOX_HEADS, HEAD_DIM)
        v_past = cache_fox_v[l, page_table].reshape(db, -1, FOX_HEADS, HEAD_DIM)
        lf_past = cache_fox_logf[l, page_table].reshape(db, -1, FOX_HEADS).astype(jnp.float32)
        return (jnp.concatenate([k_past, k], axis=1), jnp.concatenate([v_past, v], axis=1),
                jnp.concatenate([lf_past, logf], axis=1), k_past.shape[1])

    def sample_dil(kv):
        bufs = (cache_dil_kv_0, cache_dil_kv_1, cache_dil_kv_2)
        ctx = [jnp.concatenate([buf, g_kv], axis=1) for buf, g_kv in zip(bufs, kv)]
        starts = [buf.shape[1] for buf in bufs]
        states = [c[:, g_kv.shape[1]:] for c, g_kv in zip(ctx, kv)]
        return ctx, starts, states

    y_prompt, fk_p, fv_p, flf_p, dil_p = _trunk(x_prompt, prompt_fox, prompt_dil, *weights)
    y_sample, fk_s, fv_s, flf_s, dil_s = _trunk(x_sample, sample_fox, sample_dil, *weights)
    return (y_prompt, y_sample, fk_p, fv_p, flf_p, dil_p[0], dil_p[1], dil_p[2],
            fk_s, fv_s, flf_s, dil_s[0], dil_s[1], dil_s[2])
```

```python
import functools

import jax
import jax.numpy as jnp
from jax import lax
from jax.experimental import pallas as pl
from jax.experimental.pallas import tpu as pltpu

F32 = jnp.float32
BF16 = jnp.bfloat16

HEAD_DIM = 64
LANES = 128
HEADS_PER_SLAB = LANES // HEAD_DIM
DIL_WINDOWS = (128, 512, 2048)
DIL_RATES = (1, 4, 16)
N_DIL_GROUPS = len(DIL_WINDOWS)
DIL_KEYS = 128
N_EXPERT_GROUPS = 4
EXPERTS_PER_GROUP = 8
N_EXPERTS = N_EXPERT_GROUPS * EXPERTS_PER_GROUP
LN_EPS = 1e-5
ATTN_SCALE = HEAD_DIM ** -0.5
NEG = -1e30
VMEM_LIMIT = 48 * 1024 * 1024


def _params(*sem):
    return pltpu.CompilerParams(dimension_semantics=sem, vmem_limit_bytes=VMEM_LIMIT)


def _dot(a, b):
    return jnp.dot(a, b, preferred_element_type=F32)


def _dot_nt(a, b):
    return lax.dot_general(a, b, (((1,), (1,)), ((), ())), preferred_element_type=F32)


def _split3(x):
    hi = x.astype(BF16)
    r1 = x - hi.astype(F32)
    mid = r1.astype(BF16)
    lo = (r1 - mid.astype(F32)).astype(BF16)
    return hi, mid, lo


def _layer_norm(z, g, b):
    mu = jnp.mean(z, axis=-1, keepdims=True)
    zc = z - mu
    var = jnp.mean(zc * zc, axis=-1, keepdims=True)
    return zc * lax.rsqrt(var + LN_EPS) * g + b


def _seg_sum(x, seg):
    return jnp.sum(x.reshape(x.shape[0] // seg, seg, x.shape[1]), axis=1)


def _seg_expand(x, seg):
    h, l = x.shape
    return jnp.broadcast_to(x[:, None, :], (h, seg, l)).reshape(h * seg, l)


def _lane_col(x, idx):
    lane = lax.broadcasted_iota(jnp.int32, x.shape, 1)
    return jnp.sum(jnp.where(lane == idx, x, 0.0), axis=1, keepdims=True)


def _put_lane(ref_val, col, idx):
    lane = lax.broadcasted_iota(jnp.int32, ref_val.shape, 1)
    return jnp.where(lane == idx, col, ref_val)


def _upper_tri():
    r = jnp.arange(LANES)
    return (r[:, None] <= r[None, :]).astype(BF16)


def _mm_kernel(x_ref, w_ref, o_ref, *, scale):
    acc = _dot(x_ref[...].astype(BF16), w_ref[...])
    if scale is not None:
        acc = acc * scale
    o_ref[...] = acc.astype(o_ref.dtype)


def _mm(x, w, out_dtype, scale=None, tm=1024, tn=512):
    n, k = x.shape
    m = w.shape[1]
    tm, tn = min(tm, n), min(tn, m)
    return pl.pallas_call(
        functools.partial(_mm_kernel, scale=scale),
        grid=(n // tm, m // tn),
        in_specs=[pl.BlockSpec((tm, k), lambda i, j: (i, 0)),
                  pl.BlockSpec((k, tn), lambda i, j: (0, j))],
        out_specs=pl.BlockSpec((tm, tn), lambda i, j: (i, j)),
        out_shape=jax.ShapeDtypeStruct((n, m), out_dtype),
        compiler_params=_params("parallel", "parallel"),
        name="mm",
    )(x, w)


def _mm_t_kernel(w_ref, x_ref, *o_refs):
    acc = _dot_nt(w_ref[...], x_ref[0].astype(BF16))
    for o_ref in o_refs:
        o_ref[0] = acc.astype(o_ref.dtype)


def _mm_t(w_t, x, out_dtypes, row_start=0, rows=None, tm=512, tn=512):
    b, t, k = x.shape
    m = w_t.shape[0]
    rows = t - row_start if rows is None else rows
    tm, tn = min(tm, rows), min(tn, m)
    off = row_start // tm
    return pl.pallas_call(
        _mm_t_kernel,
        grid=(b, rows // tm, m // tn),
        in_specs=[pl.BlockSpec((tn, k), lambda bi, i, j: (j, 0)),
                  pl.BlockSpec((1, tm, k), lambda bi, i, j: (bi, i + off, 0))],
        out_specs=[pl.BlockSpec((1, tn, tm), lambda bi, i, j: (bi, j, i)) for _ in out_dtypes],
        out_shape=[jax.ShapeDtypeStruct((b, m, rows), dt) for dt in out_dtypes],
        compiler_params=_params("parallel", "parallel", "parallel"),
        name="mm_t",
    )(w_t, x)


def _gate_kernel(x_ref, wf_ref, bf_ref, u_ref, lf_ref, cum_ref, carry_ref):
    @pl.when(pl.program_id(1) == 0)
    def _():
        carry_ref[...] = jnp.zeros_like(carry_ref)

    y = _dot_nt(wf_ref[...], x_ref[0].astype(BF16)) + bf_ref[...]
    lf = jnp.minimum(y, 0.0) - jnp.log1p(jnp.exp(-jnp.abs(y)))
    lf_ref[0] = lf
    u = u_ref[...]
    carry = carry_ref[:, 0:1]
    for c in range(lf.shape[1] // LANES):
        hi, mid, lo = _split3(lf[:, c * LANES:(c + 1) * LANES])
        cs = _dot(hi, u) + _dot(mid, u) + _dot(lo, u) + carry
        cum_ref[0, :, c * LANES:(c + 1) * LANES] = cs
        carry = cs[:, LANES - 1:LANES]
    carry_ref[...] = jnp.broadcast_to(carry, carry_ref.shape)


def _gate(x, wf_t, b_f, tm=1024):
    b, t, k = x.shape
    h = wf_t.shape[0]
    tm = min(tm, t)
    return pl.pallas_call(
        _gate_kernel,
        grid=(b, t // tm),
        in_specs=[pl.BlockSpec((1, tm, k), lambda bi, i: (bi, i, 0)),
                  pl.BlockSpec((h, k), lambda bi, i: (0, 0)),
                  pl.BlockSpec((h, 1), lambda bi, i: (0, 0)),
                  pl.BlockSpec((LANES, LANES), lambda bi, i: (0, 0))],
        out_specs=[pl.BlockSpec((1, h, tm), lambda bi, i: (bi, 0, i))] * 2,
        out_shape=[jax.ShapeDtypeStruct((b, h, t), F32)] * 2,
        scratch_shapes=[pltpu.VMEM((h, LANES), F32)],
        compiler_params=_params("arbitrary", "arbitrary"),
        name="gate",
    )(x, wf_t, b_f, _upper_tri())


def _fox_attn_kernel(q_ref, kt_ref, vt_ref, c_ref, o_ref, qm_ref, *, tq):
    iq = pl.program_id(2)
    q = q_ref[0]
    lane = lax.broadcasted_iota(jnp.int32, q.shape, 1)
    zero = jnp.zeros_like(q)
    qm_ref[0] = jnp.where(lane < HEAD_DIM, q, zero)
    qm_ref[1] = jnp.where(lane >= HEAD_DIM, q, zero)

    def chunk(j, carry, masked):
        start = pl.multiple_of(j * tq, tq)
        kt = kt_ref[0, :, pl.ds(start, tq)]
        new = []
        for a in range(HEADS_PER_SLAB):
            m, l, acc = carry[a]
            s = _dot(qm_ref[a], kt) - c_ref[0, 0, a:a + 1, pl.ds(start, tq)]
            if masked:
                row = lax.broadcasted_iota(jnp.int32, s.shape, 0)
                col = lax.broadcasted_iota(jnp.int32, s.shape, 1)
                s = jnp.where(col <= row, s, NEG)
            m_new = jnp.maximum(m, jnp.max(s, axis=1, keepdims=True))
            alpha = jnp.exp(m - m_new)
            p = jnp.exp(s - m_new)
            l = alpha * l + jnp.sum(p, axis=1, keepdims=True)
            vt = vt_ref[0, a * HEAD_DIM:(a + 1) * HEAD_DIM, pl.ds(start, tq)]
            acc = alpha * acc + _dot_nt(p.astype(BF16), vt)
            new.append((m_new, l, acc))
        return tuple(new)

    init = tuple((jnp.full((tq, 1), NEG, F32), jnp.zeros((tq, 1), F32), jnp.zeros((tq, HEAD_DIM), F32))
                 for _ in range(HEADS_PER_SLAB))
    carry = lax.fori_loop(0, iq, lambda j, c: chunk(j, c, False), init)
    carry = chunk(iq, carry, True)
    o_ref[0] = jnp.concatenate([acc / l for (_, l, acc) in carry], axis=1).astype(o_ref.dtype)


def _fox_attn(q, kt, vt, cum_t, tq=512):
    b, t, d = q.shape
    n_slab = d // LANES
    tq = min(tq, t)
    c4 = cum_t.reshape(b, n_slab, HEADS_PER_SLAB, t)
    return pl.pallas_call(
        functools.partial(_fox_attn_kernel, tq=tq),
        grid=(b, n_slab, t // tq),
        in_specs=[pl.BlockSpec((1, tq, LANES), lambda bi, p, i: (bi, i, p)),
                  pl.BlockSpec((1, LANES, t), lambda bi, p, i: (bi, p, 0)),
                  pl.BlockSpec((1, LANES, t), lambda bi, p, i: (bi, p, 0)),
                  pl.BlockSpec((1, 1, HEADS_PER_SLAB, t), lambda bi, p, i: (bi, p, 0, 0))],
        out_specs=pl.BlockSpec((1, tq, LANES), lambda bi, p, i: (bi, i, p)),
        out_shape=jax.ShapeDtypeStruct((b, t, d), BF16),
        scratch_shapes=[pltpu.VMEM((HEADS_PER_SLAB, tq, LANES), BF16)],
        compiler_params=_params("parallel", "parallel", "arbitrary"),
        name="fox_attn",
    )(q, kt, vt, c4)


def _proj_norm_kernel(a_ref, w_ref, x_ref, g_ref, b_ref, o_ref, *, alpha):
    mix = _dot(a_ref[...], w_ref[...])
    o_ref[...] = _layer_norm(alpha * x_ref[...] + mix, g_ref[...], b_ref[...])


def _proj_norm(a, w, x, g, b, alpha, tm=512):
    n, ka = a.shape
    d = w.shape[1]
    tm = min(tm, n)
    return pl.pallas_call(
        functools.partial(_proj_norm_kernel, alpha=alpha),
        grid=(n // tm,),
        in_specs=[pl.BlockSpec((tm, ka), lambda i: (i, 0)),
                  pl.BlockSpec((ka, d), lambda i: (0, 0)),
                  pl.BlockSpec((tm, d), lambda i: (i, 0)),
                  pl.BlockSpec((1, d), lambda i: (0, 0)),
                  pl.BlockSpec((1, d), lambda i: (0, 0))],
        out_specs=pl.BlockSpec((tm, d), lambda i: (i, 0)),
        out_shape=jax.ShapeDtypeStruct((n, d), F32),
        compiler_params=_params("parallel"),
        name="proj_norm",
    )(a, w, x, g, b)


def _route(logits):
    lane = lax.broadcasted_iota(jnp.int32, logits.shape, 1).astype(F32)
    far = float(LANES)
    gl = jnp.where(lane < N_EXPERT_GROUPS, logits, NEG)
    gmax = jnp.max(gl, axis=1, keepdims=True)
    gsel = jnp.min(jnp.where(gl == gmax, lane, far), axis=1, keepdims=True)
    psel = 1.0 / jnp.sum(jnp.exp(gl - gmax), axis=1, keepdims=True)
    lo = N_EXPERT_GROUPS + gsel * EXPERTS_PER_GROUP
    el = jnp.where((lane >= lo) & (lane < lo + EXPERTS_PER_GROUP), logits, NEG)
    v1 = jnp.max(el, axis=1, keepdims=True)
    i1 = jnp.min(jnp.where(el == v1, lane, far), axis=1, keepdims=True)
    el2 = jnp.where(lane == i1, NEG, el)
    v2 = jnp.max(el2, axis=1, keepdims=True)
    i2 = jnp.min(jnp.where(el2 == v2, lane, far), axis=1, keepdims=True)
    e2 = jnp.exp(v2 - v1)
    den = 1.0 + e2
    return jnp.where(lane == i1, psel / den, jnp.where(lane == i2, psel * e2 / den, 0.0))


def _moe_kernel(x_ref, wr_ref, br_ref, wg_ref, wu_ref, wd_ref, g_ref, b_ref, o_ref,
                xb_ref, gates_ref, acc_ref, *, alpha):
    e = pl.program_id(1)

    @pl.when(e == 0)
    def _():
        x = x_ref[...]
        xh = x.astype(BF16)
        xl = (x - xh.astype(F32)).astype(BF16)
        wr = wr_ref[...]
        wh = wr.astype(BF16)
        wl = (wr - wh.astype(F32)).astype(BF16)
        xb_ref[...] = xh
        logits = _dot(xh, wh) + _dot(xl, wh) + _dot(xh, wl) + br_ref[...]
        gates_ref[...] = _route(logits)
        acc_ref[...] = jnp.zeros_like(acc_ref)

    xb = xb_ref[...]
    hg = _dot(xb, wg_ref[0])
    hu = _dot(xb, wu_ref[0])
    gcol = _lane_col(gates_ref[...], e + N_EXPERT_GROUPS)
    h = hg * (1.0 / (1.0 + jnp.exp(-hg))) * hu
    acc_ref[...] += _dot((h * gcol).astype(BF16), wd_ref[0])

    @pl.when(e == pl.num_programs(1) - 1)
    def _():
        o_ref[...] = _layer_norm(alpha * x_ref[...] + acc_ref[...], g_ref[...], b_ref[...])


def _moe_norm(x, wr, br, wg, wu, wd, g, b, alpha, tm=1024):
    n, d = x.shape
    ne, _, f = wg.shape
    tm = min(tm, n)
    return pl.pallas_call(
        functools.partial(_moe_kernel, alpha=alpha),
        grid=(n // tm, ne),
        in_specs=[pl.BlockSpec((tm, d), lambda i, e: (i, 0)),
                  pl.BlockSpec((d, LANES), lambda i, e: (0, 0)),
                  pl.BlockSpec((1, LANES), lambda i, e: (0, 0)),
                  pl.BlockSpec((1, d, f), lambda i, e: (e, 0, 0)),
                  pl.BlockSpec((1, d, f), lambda i, e: (e, 0, 0)),
                  pl.BlockSpec((1, f, d), lambda i, e: (e, 0, 0)),
                  pl.BlockSpec((1, d), lambda i, e: (0, 0)),
                  pl.BlockSpec((1, d), lambda i, e: (0, 0))],
        out_specs=pl.BlockSpec((tm, d), lambda i, e: (i, 0)),
        out_shape=jax.ShapeDtypeStruct((n, d), F32),
        scratch_shapes=[pltpu.VMEM((tm, d), BF16), pltpu.VMEM((tm, LANES), F32), pltpu.VMEM((tm, d), F32)],
        compiler_params=_params("parallel", "arbitrary"),
        name="moe_norm",
    )(x, wr, br, wg, wu, wd, g, b)


def _alibi_slope(g, h, n_heads):
    return 2.0 ** (-8.0 * (h * N_DIL_GROUPS + g + 1) / (N_DIL_GROUPS * n_heads))


def _dil_attn_kernel(q_ref, kc_ref, kp_ref, vc_ref, vp_ref, o_ref, lse_ref, kbuf, vbuf, *, tu, rate, slopes):
    i = pl.program_id(2)
    kbuf[0:DIL_KEYS] = kp_ref[0]
    kbuf[DIL_KEYS:] = kc_ref[0]
    vbuf[0:DIL_KEYS] = vp_ref[0]
    vbuf[DIL_KEYS:] = vc_ref[0]
    shape = (DIL_KEYS, 2 * DIL_KEYS)
    a = lax.broadcasted_iota(jnp.int32, shape, 0)
    c = lax.broadcasted_iota(jnp.int32, shape, 1)
    dist = a - c + DIL_KEYS
    base = jnp.where((dist >= 0) & (dist <= DIL_KEYS), (-rate * dist).astype(F32), NEG)
    base_first = jnp.where(c >= DIL_KEYS, base, NEG)
    lane = lax.broadcasted_iota(jnp.int32, (DIL_KEYS, LANES), 1)
    left = lane < HEAD_DIM
    for s in range(tu // DIL_KEYS):
        bias = jnp.where(i == 0, base_first, base) if s == 0 else base
        r0 = s * DIL_KEYS
        for p in range(q_ref.shape[2] // LANES):
            cs = slice(p * LANES, (p + 1) * LANES)
            qp = q_ref[0, r0:r0 + DIL_KEYS, cs]
            kk = kbuf[r0:r0 + 2 * DIL_KEYS, cs]
            vv = vbuf[r0:r0 + 2 * DIL_KEYS, cs]
            outs, lses = [], []
            for hh in range(HEADS_PER_SLAB):
                qm = jnp.where(left if hh == 0 else jnp.logical_not(left), qp, jnp.zeros_like(qp))
                sc = _dot_nt(qm, kk) + slopes[HEADS_PER_SLAB * p + hh] * bias
                m = jnp.max(sc, axis=1, keepdims=True)
                e = jnp.exp(sc - m)
                l = jnp.sum(e, axis=1, keepdims=True)
                outs.append(_dot(e.astype(BF16), vv) / l)
                lses.append(m + jnp.log(l))
            o_ref[0, r0:r0 + DIL_KEYS, cs] = jnp.where(left, outs[0], outs[1])
            lse_ref[0, r0:r0 + DIL_KEYS, cs] = jnp.where(left, lses[0], lses[1])


def _dil_attn(q, kv, g, tu=512):
    b, t, dq = q.shape
    dg = dq // N_DIL_GROUPS
    rate = DIL_RATES[g]
    u = t // rate
    tu = min(tu, u)
    qv = q.reshape(b, u, rate * dq)
    kvv = kv.reshape(b, u, rate * 2 * dq)
    nq, nkv = dq // dg, 2 * dq // dg
    ratio = tu // DIL_KEYS
    slopes = tuple(_alibi_slope(g, h, dg // HEAD_DIM) for h in range(dg // HEAD_DIM))
    cur =lambda col: (lambda bi, r, i: (bi, i, r * nkv + col))
    prev = lambda col: (lambda bi, r, i: (bi, jnp.maximum(i * ratio - 1, 0), r * nkv + col))
    o, lse = pl.pallas_call(
        functools.partial(_dil_attn_kernel, tu=tu, rate=rate, slopes=slopes),
        grid=(b, rate, u // tu),
        in_specs=[pl.BlockSpec((1, tu, dg), lambda bi, r, i: (bi, i, r * nq + g)),
                  pl.BlockSpec((1, tu, dg), cur(2 * g)),
                  pl.BlockSpec((1, DIL_KEYS, dg), prev(2 * g)),
                  pl.BlockSpec((1, tu, dg), cur(2 * g + 1)),
                  pl.BlockSpec((1, DIL_KEYS, dg), prev(2 * g + 1))],
        out_specs=[pl.BlockSpec((1, tu, dg), lambda bi, r, i: (bi, i, r))] * 2,
        out_shape=[jax.ShapeDtypeStruct((b, u, rate * dg), F32)] * 2,
        scratch_shapes=[pltpu.VMEM((tu + DIL_KEYS, dg), BF16)] * 2,
        compiler_params=_params("parallel", "parallel", "arbitrary"),
        name=f"dil_attn{g}",
    )(qv, kvv, kvv, kvv, kvv)
    return o.reshape(b * t, dg), lse.reshape(b * t, dg)


def _mix_norm_kernel(o0, o1, o2, l0, l1, l2, w_ref, x_ref, g_ref, b_ref, out_ref, *, alpha):
    ls = [l0[...], l1[...], l2[...]]
    m = jnp.maximum(jnp.maximum(ls[0], ls[1]), ls[2])
    es = [jnp.exp(l - m) for l in ls]
    o = (es[0] * o0[...] + es[1] * o1[...] + es[2] * o2[...]) / (es[0] + es[1] + es[2])
    mix = _dot(o.astype(BF16), w_ref[...])
    out_ref[...] = _layer_norm(alpha * x_ref[...] + mix, g_ref[...], b_ref[...])


def _mix_norm(os, ls, w, x, g, b, alpha, tm=512):
    n, d = x.shape
    dg = w.shape[0]
    tm = min(tm, n)
    row = lambda width: pl.BlockSpec((tm, width), lambda i: (i, 0))
    fixed = lambda shape: pl.BlockSpec(shape, lambda i: (0, 0))
    return pl.pallas_call(
        functools.partial(_mix_norm_kernel, alpha=alpha),
        grid=(n // tm,),
        in_specs=[row(dg)] * 6 + [fixed((dg, d)), row(d), fixed((1, d)), fixed((1, d))],
        out_specs=row(d),
        out_shape=jax.ShapeDtypeStruct((n, d), F32),
        compiler_params=_params("parallel"),
        name="mix_norm",
    )(*os, *ls, w, x, g, b)


def _fox_decode_kernel(pt_ref, qt_ref, u_ref, *refs, pages):
    k_refs, v_refs, lf_refs = refs[:pages], refs[pages:2 * pages], refs[2 * pages:3 * pages]
    ot_ref, mt_ref, lt_ref, ct_ref, qcol_ref, acc_ref, m_ref, l_ref, c_ref = refs[3 * pages:]
    b, j = pl.program_id(0), pl.program_id(1)
    nh = m_ref.shape[0]

    @pl.when((b == 0) & (j == 0))
    def _():
        for r in (ot_ref, mt_ref, lt_ref, ct_ref):
            r[...] = jnp.zeros_like(r)

    @pl.when(j == 0)
    def _():
        qcol = _lane_col(qt_ref[...], b) * ATTN_SCALE
        qcol_ref[...] = jnp.broadcast_to(qcol.astype(BF16).astype(F32), qcol_ref.shape)
        acc_ref[...] = jnp.zeros_like(acc_ref)
        m_ref[...] = jnp.full_like(m_ref, NEG)
        l_ref[...] = jnp.zeros_like(l_ref)
        c_ref[...] = jnp.zeros_like(c_ref)

    u = u_ref[...]
    for pi in range(pages):
        hi, mid, lo = _split3(lf_refs[pi][0])
        cs = _dot(hi, u) + _dot(mid, u) + _dot(lo, u) + c_ref[...]
        c_ref[...] = jnp.broadcast_to(cs[:, LANES - 1:LANES], c_ref.shape)
        s = _seg_sum(qcol_ref[...] * k_refs[pi][0], HEAD_DIM) - cs
        m_old = m_ref[...]
        m_new = jnp.maximum(m_old, jnp.max(s, axis=1, keepdims=True))
        alpha = jnp.exp(m_old - m_new)
        p = jnp.exp(s - m_new)
        m_ref[...] = m_new
        l_ref[...] = alpha * l_ref[...] + p
        acc_ref[...] = (acc_ref[...] * _seg_expand(alpha, HEAD_DIM)
                        + _seg_expand(p, HEAD_DIM) * v_refs[pi][0])

    @pl.when(j == pl.num_programs(1) - 1)
    def _():
        ot_ref[...] = _put_lane(ot_ref[...], jnp.sum(acc_ref[...], axis=1, keepdims=True), b)
        lt_ref[...] = _put_lane(lt_ref[...], jnp.sum(l_ref[...], axis=1, keepdims=True), b)
        mt_ref[...] = _put_lane(mt_ref[...], m_ref[:, 0:1], b)
        ct_ref[...] = _put_lane(ct_ref[...], c_ref[:, 0:1], b)


def _fox_decode(page_table, q_t, k_pages, v_pages, lf_pages, pages=4):
    d, nb = q_t.shape
    nh = lf_pages.shape[1]
    n_pages = page_table.shape[1]
    ps = k_pages.shape[2]
    pt = page_table.reshape(-1)

    def page_spec(rows, pi):
        return pl.BlockSpec((1, rows, ps), lambda b, j, pt_ref: (pt_ref[b * n_pages + j * pages + pi], 0, 0))

    fixed = lambda shape: pl.BlockSpec(shape, lambda b, j, pt_ref: (0, 0))
    grid_spec = pltpu.PrefetchScalarGridSpec(
        num_scalar_prefetch=1,
        grid=(nb, n_pages // pages),
        in_specs=([fixed((d, nb)), fixed((LANES, LANES))]
                  + [page_spec(d, pi) for pi in range(pages)] * 2
                  + [page_spec(nh, pi) for pi in range(pages)]),
        out_specs=[fixed((d, nb)), fixed((nh, nb)), fixed((nh, nb)), fixed((nh, nb))],
        scratch_shapes=[pltpu.VMEM((d, ps), F32), pltpu.VMEM((d, ps), F32),
                        pltpu.VMEM((nh, ps), F32), pltpu.VMEM((nh, ps), F32), pltpu.VMEM((nh, ps), F32)],
    )
    return pl.pallas_call(
        functools.partial(_fox_decode_kernel, pages=pages),
        grid_spec=grid_spec,
        out_shape=[jax.ShapeDtypeStruct((d, nb), F32)] + [jax.ShapeDtypeStruct((nh, nb), F32)] * 3,
        compiler_params=_params("arbitrary", "arbitrary"),
        name="fox_decode",
    )(pt, q_t, _upper_tri(), *([k_pages] * pages), *([v_pages] * pages), *([lf_pages] * pages))


def _fox_merge_kernel(ot_ref, mt_ref, lt_ref, ct_ref, qt_ref, kn_ref, vn_ref, lfn_ref,
                      w_ref, x_ref, g_ref, b_ref, out_ref, *, alpha):
    qs = (qt_ref[...] * ATTN_SCALE).astype(BF16).astype(F32)
    s_new = _seg_sum(qs * kn_ref[...], HEAD_DIM) - (ct_ref[...] + lfn_ref[...])
    m_p = mt_ref[...]
    m = jnp.maximum(m_p, s_new)
    a_p = jnp.exp(m_p - m)
    a_n = jnp.exp(s_new - m)
    l = lt_ref[...] * a_p + a_n
    o_t = (ot_ref[...] * _seg_expand(a_p, HEAD_DIM) + _seg_expand(a_n, HEAD_DIM) * vn_ref[...]) \
        / _seg_expand(l, HEAD_DIM)
    mix = _dot(o_t.T.astype(BF16), w_ref[...])
    out_ref[...] = _layer_norm(alpha * x_ref[...] + mix, g_ref[...], b_ref[...])


def _fox_merge(o_t, m_t, l_t, c_t, q_t, kn_t, vn_t, lfn_t, w, x, g, b, alpha):
    return pl.pallas_call(
        functools.partial(_fox_merge_kernel, alpha=alpha),
        out_shape=jax.ShapeDtypeStruct(x.shape, F32),
        compiler_params=pltpu.CompilerParams(vmem_limit_bytes=VMEM_LIMIT),
        name="fox_merge",
    )(o_t, m_t, l_t, c_t, q_t, kn_t, vn_t, lfn_t, w, x, g, b)


def _dil_decode_kernel(buf_ref, qt_ref, kvn_ref, sl_ref, out_ref, ot_ref, mt_ref, lt_ref, p_ref,
                       *, window, rate):
    b, c = pl.program_id(0), pl.program_id(1)

    @pl.when((b == 0) & (c == 0))
    def _():
        for r in (ot_ref, mt_ref, lt_ref):
            r[...] = jnp.zeros_like(r)

    x = buf_ref[0]
    t = lax.broadcasted_iota(jnp.int32, (1, window), 1)

    @pl.when(c == 0)
    def _():
        qcol = (_lane_col(qt_ref[...], b) * ATTN_SCALE).astype(BF16).astype(F32)
        s = _seg_sum(x * qcol, HEAD_DIM)
        dist = (window - t).astype(F32)
        s = jnp.where((t & (rate - 1)) == 0, s - sl_ref[:, 0:1] * dist, NEG)
        m = jnp.max(s, axis=1, keepdims=True)
        p = jnp.exp(s - m)
        p_ref[...] = p
        mt_ref[...] = _put_lane(mt_ref[...], m, b)
        lt_ref[...] = _put_lane(lt_ref[...], jnp.sum(p, axis=1, keepdims=True), b)

    @pl.when(c == 1)
    def _():
        o = jnp.sum(x * _seg_expand(p_ref[...], HEAD_DIM), axis=1, keepdims=True)
        ot_ref[...] = _put_lane(ot_ref[...], o, b)

    newcol = _lane_col(kvn_ref[...], b)
    shifted = pltpu.roll(x, window - 1, axis=1)
    out_ref[0] = jnp.where(t == window - 1, newcol, shifted)


def _dil_decode(buf_t, qg_t, kvn_t, g):
    nb, _, window = buf_t.shape
    dg = qg_t.shape[0]
    nh = dg // HEAD_DIM
    rate = DIL_RATES[g]
    assert rate & (rate - 1) == 0 and window == DIL_KEYS * rate
    slopes = jnp.asarray([_alibi_slope(g, h, nh) for h in range(nh)], F32)
    sl = jnp.broadcast_to(slopes.reshape(nh, 1), (nh, LANES))
    fixed = lambda shape: pl.BlockSpec(shape, lambda b, c: (0, 0))
    return pl.pallas_call(
        functools.partial(_dil_decode_kernel, window=window, rate=rate),
        grid=(nb, 2),
        in_specs=[pl.BlockSpec((1, dg, window), lambda b, c: (b, c, 0)),
                  fixed((dg, nb)),
                  pl.BlockSpec((dg, nb), lambda b, c: (c, 0)),
                  fixed((nh, LANES))],
        out_specs=[pl.BlockSpec((1, dg, window), lambda b, c: (b, c, 0)),
                   fixed((dg, nb)), fixed((nh, nb)), fixed((nh, nb))],
        out_shape=[jax.ShapeDtypeStruct(buf_t.shape, F32),
                   jax.ShapeDtypeStruct((dg, nb), F32),
                   jax.ShapeDtypeStruct((nh, nb), F32),
                   jax.ShapeDtypeStruct((nh, nb), F32)],
        scratch_shapes=[pltpu.VMEM((nh, window), F32)],
        compiler_params=_params("arbitrary", "arbitrary"),
        name=f"dil_decode{g}",
    )(buf_t, qg_t, kvn_t, sl)


def _dil_merge_kernel(*refs, alpha):
    states = refs[:3 * N_DIL_GROUPS]
    qt_ref, kvn_ref, w_ref, x_ref, g_ref, b_ref, out_ref = refs[3 * N_DIL_GROUPS:]
    dg = w_ref.shape[0]
    outs, lses = [], []
    for g in range(N_DIL_GROUPS):
        ot_ref, mt_ref, lt_ref = states[3 * g:3 * g + 3]
        qs =(qt_ref[g * dg:(g + 1) * dg, :] * ATTN_SCALE).astype(BF16).astype(F32)
        kn = kvn_ref[2 * g * dg:(2 * g + 1) * dg, :]
        vn = kvn_ref[(2 * g + 1) * dg:(2 * g + 2) * dg, :]
        s_new = _seg_sum(qs * kn, HEAD_DIM)
        m_p = mt_ref[...]
        m = jnp.maximum(m_p, s_new)
        a_p = jnp.exp(m_p - m)
        a_n = jnp.exp(s_new - m)
        l = lt_ref[...] * a_p + a_n
        o = (ot_ref[...] * _seg_expand(a_p, HEAD_DIM) + _seg_expand(a_n, HEAD_DIM) * vn) \
            / _seg_expand(l, HEAD_DIM)
        outs.append(o)
        lses.append(m + jnp.log(l))
    m = jnp.maximum(jnp.maximum(lses[0], lses[1]), lses[2])
    es = [jnp.exp(l - m) for l in lses]
    tot = es[0] + es[1] + es[2]
    o_t = sum(_seg_expand(e / tot, HEAD_DIM) * o for e, o in zip(es, outs))
    mix = _dot(o_t.T.astype(BF16), w_ref[...])
    out_ref[...] = _layer_norm(alpha * x_ref[...] + mix, g_ref[...], b_ref[...])


def _dil_merge(states, q_t, kvn_t, w, x, g, b, alpha):
    flat = [a for st in states for a in st]
    return pl.pallas_call(
        functools.partial(_dil_merge_kernel, alpha=alpha),
        out_shape=jax.ShapeDtypeStruct(x.shape, F32),
        compiler_params=pltpu.CompilerParams(vmem_limit_bytes=VMEM_LIMIT),
        name="dil_merge",
    )(*flat, q_t, kvn_t, w, x, g, b)


def kernel(x_prompt, x_sample, cache_fox_k, cache_fox_v, cache_fox_logf, cache_dil_kv_0, cache_dil_kv_1,
           cache_dil_kv_2, page_table, w_qkvf_a, b_f_a, w_o_a, w_q_b, w_kv_b, w_o_b, ln_g, ln_b,
           w_router_grp, b_router_grp, w_router_exp, b_router_exp, w_gate, w_up, w_down):
    nb, t, d = x_prompt.shape
    ns = x_sample.shape[0]
    depth = w_gate.shape[0]
    alpha = (2 * depth) ** 0.25
    nh = d // HEAD_DIM
    dg = w_o_b.shape[1]
    dq = N_DIL_GROUPS * dg

    w_a = w_qkvf_a[0]
    wq = w_a[:, :d].astype(BF16)
    wqkv_t = w_a[:, :3 * d].T.astype(BF16)
    wf_t = w_a[:, 3 * d:].T.astype(BF16)
    b_f = b_f_a[0].reshape(nh, 1)
    wo_a = w_o_a[0].astype(BF16)
    wqb = w_q_b[0].astype(BF16)
    wkv = w_kv_b.astype(BF16)
    wqkv_b_t = jnp.concatenate([w_q_b[0], w_kv_b], axis=1).T.astype(BF16)
    wkv_t = w_kv_b.T.astype(BF16)
    wo_b = w_o_b[0].astype(BF16)
    pad = LANES - N_EXPERT_GROUPS - N_EXPERTS
    wr = [jnp.pad(jnp.concatenate([w_router_grp[l], w_router_exp[l]], axis=1), ((0, 0), (0, pad)))
          for l in range(depth)]
    br = [jnp.pad(jnp.concatenate([b_router_grp[l], b_router_exp[l]]), (0, pad)).reshape(1, LANES)
          for l in range(depth)]
    wg = [w_gate[l].astype(BF16) for l in range(depth)]
    wu = [w_up[l].astype(BF16) for l in range(depth)]
    wd = [w_down[l].astype(BF16) for l in range(depth)]
    lng = lambda l, i: ln_g[l, i].reshape(1, d)
    lnb = lambda l, i: ln_b[l, i].reshape(1, d)

    def moe(x, l):
        return _moe_norm(x, wr[l], br[l], wg[l], wu[l], wd[l], lng(l, 1), lnb(l, 1), alpha)

    xp = x_prompt.reshape(nb * t, d)
    q = _mm(xp, wq, BF16, scale=ATTN_SCALE).reshape(nb, t, d)
    k_t, k_tb = _mm_t(wqkv_t[d:2 * d], x_prompt, (F32, BF16))
    v_t, v_tb = _mm_t(wqkv_t[2 * d:], x_prompt, (F32, BF16))
    lf_t, cum_t = _gate(x_prompt, wf_t, b_f)
    o = _fox_attn(q, k_tb, v_tb, cum_t)
    x1 = _proj_norm(o.reshape(nb * t, d), wo_a, xp, lng(0, 0), lnb(0, 0), alpha)
    x2 = moe(x1, 0)

    qb = _mm(x2, wqb, BF16, scale=ATTN_SCALE).reshape(nb, t, dq)
    kvb = _mm(x2, wkv, BF16).reshape(nb, t, 2 * dq)
    tail = min(max(DIL_WINDOWS), t)
    (kv_tail_t,) = _mm_t(wkv_t, x2.reshape(nb, t, d), (F32,), row_start=t - tail, rows=tail)
    os_, ls_ = zip(*[_dil_attn(qb, kvb, g) for g in range(N_DIL_GROUPS)])
    x3 = _mix_norm(os_, ls_, wo_b, x2, lng(1, 0), lnb(1, 0), alpha)
    y_prompt = moe(x3, 1).reshape(nb, t, d)

    def feat_major_out(a_t, lead):
        b_, f_, t_ = a_t.shape
        return a_t.reshape(*lead, b_, f_ // HEAD_DIM, HEAD_DIM, t_).transpose(
            *range(len(lead)), len(lead), len(lead) + 3, len(lead) + 1, len(lead) + 2)

    fox_k_p = feat_major_out(k_t, (1,))
    fox_v_p = feat_major_out(v_t, (1,))
    fox_lf_p = lf_t.transpose(0, 2, 1)[None]
    dil_p = []
    for g, w in enumerate(DIL_WINDOWS):
        wlen = min(w, t)
        sl = kv_tail_t[:, 2 * g * dg:(2 * g + 2) * dg, tail - wlen:]
        dil_p.append(sl.reshape(nb, 2, dg // HEAD_DIM, HEAD_DIM, wlen).transpose(0, 4, 1, 2, 3))

    xs = x_sample.reshape(ns, d)
    (qkv_s,) = _mm_t(wqkv_t, xs[None], (F32,))
    q_st, kn_t, vn_t = qkv_s[0, :d], qkv_s[0, d:2 * d], qkv_s[0, 2 * d:]
    lfn_t, _ = _gate(xs[None], wf_t, b_f)
    lfn_t = lfn_t[0]
    n_phys, ps = cache_fox_k.shape[1], cache_fox_k.shape[2]
    k_pages = cache_fox_k[0].transpose(0, 2, 3, 1).reshape(n_phys, d, ps)
    v_pages = cache_fox_v[0].transpose(0, 2, 3, 1).reshape(n_phys, d, ps)
    lf_pages = cache_fox_logf[0].transpose(0, 2, 1)
    o_t, m_t, l_t, c_t = _fox_decode(page_table, q_st, k_pages, v_pages, lf_pages)
    x1s = _fox_merge(o_t, m_t, l_t, c_t, q_st, kn_t, vn_t, lfn_t, wo_a, xs, lng(0, 0), lnb(0, 0), alpha)
    x2s = moe(x1s, 0)

    (qkv_b,) = _mm_t(wqkv_b_t, x2s[None], (F32,))
    qb_t, kvn_t = qkv_b[0, :dq], qkv_b[0, dq:]
    states, dil_s = [], []
    for g, cache in enumerate((cache_dil_kv_0, cache_dil_kv_1, cache_dil_kv_2)):
        wlen = cache.shape[1]
        buf_t = cache.transpose(0, 2, 3, 4, 1).reshape(ns, 2 * dg, wlen)
        new_buf, og, mg, lg = _dil_decode(buf_t, qb_t[g * dg:(g + 1) * dg], kvn_t[2 * g * dg:(2 * g + 2) * dg], g)
        states.append((og, mg, lg))
        dil_s.append(new_buf.reshape(ns, 2, dg // HEAD_DIM, HEAD_DIM, wlen).transpose(0, 4, 1, 2, 3))
    x3s = _dil_merge(states, qb_t, kvn_t, wo_b, x2s, lng(1, 0), lnb(1, 0), alpha)
    y_sample = moe(x3s, 1).reshape(ns, 1, d)

    fox_k_s = kn_t.reshape(1, 1, nh, HEAD_DIM, ns).transpose(0, 4, 1, 2, 3)
    fox_v_s = vn_t.reshape(1, 1, nh, HEAD_DIM, ns).transpose(0, 4, 1, 2, 3)
    fox_lf_s = lfn_t.reshape(1, 1, nh, ns).transpose(0, 3, 1, 2)

    return (y_prompt, y_sample, fox_k_p, fox_v_p, fox_lf_p, dil_p[0], dil_p[1], dil_p[2],
            fox_k_s, fox_v_s, fox_lf_s, dil_s[0], dil_s[1], dil_s[2])
```

```python
import functools

import jax
import jax.numpy as jnp
from jax import lax
from jax.experimental import pallas as pl
from jax.experimental.pallas import tpu as pltpu

F32 = jnp.float32
BF16 = jnp.bfloat16

HEAD_DIM = 64
LANES = 128
HEADS_PER_SLAB = LANES // HEAD_DIM
DIL_WINDOWS = (128, 512, 2048)
DIL_RATES = (1, 4, 16)
N_DIL_GROUPS = len(DIL_WINDOWS)
DIL_KEYS = 128
N_EXPERT_GROUPS = 4
EXPERTS_PER_GROUP = 8
N_EXPERTS = N_EXPERT_GROUPS * EXPERTS_PER_GROUP
LN_EPS = 1e-5
ATTN_SCALE = HEAD_DIM ** -0.5
LOG2E = 1.4426950408889634
BF16_ROWS = 16
NEG = -1e30
VMEM_LIMIT = 48 * 1024 * 1024
DECODE_BLOCK_BYTES = 4 * 1024 * 1024


def _params(*sem):
    return pltpu.CompilerParams(dimension_semantics=sem, vmem_limit_bytes=VMEM_LIMIT)


def _dot(a, b):
    return jnp.dot(a, b, preferred_element_type=F32)


def _dot_nt(a, b):
    return lax.dot_general(a, b, (((1,), (1,)), ((), ())), preferred_element_type=F32)


def _split3(x):
    hi = x.astype(BF16)
    r1 = x - hi.astype(F32)
    mid = r1.astype(BF16)
    lo = (r1 - mid.astype(F32)).astype(BF16)
    return hi, mid, lo


def _layer_norm(z, g, b):
    mu = jnp.mean(z, axis=-1, keepdims=True)
    zc = z - mu
    var = jnp.mean(zc * zc, axis=-1, keepdims=True)
    return zc * lax.rsqrt(var + LN_EPS) * g + b


def _seg_sum(x, seg):
    return jnp.sum(x.reshape(x.shape[0] // seg, seg, x.shape[1]), axis=1)


def _seg_expand(x, seg):
    h, l = x.shape
    return jnp.broadcast_to(x[:, None, :], (h, seg, l)).reshape(h * seg, l)


def _lane_col(x, idx):
    lane = lax.broadcasted_iota(jnp.int32, x.shape, 1)
    return jnp.sum(jnp.where(lane == idx, x, 0.0), axis=1, keepdims=True)


def _put_lane(ref_val, col, idx):
    lane = lax.broadcasted_iota(jnp.int32, ref_val.shape, 1)
    return jnp.where(lane == idx, col, ref_val)


def _upper_tri():
    r = jnp.arange(LANES)
    return (r[:, None] <= r[None, :]).astype(BF16)


def _mm_kernel(x_ref, w_ref, o_ref, *, scale):
    acc = _dot(x_ref[...].astype(BF16), w_ref[...])
    if scale is not None:
        acc = acc * scale
    o_ref[...] = acc.astype(o_ref.dtype)


def _mm(x, w, out_dtype, scale=None, tm=1024, tn=512):
    n, k = x.shape
    m = w.shape[1]
    tm, tn = min(tm, n), min(tn, m)
    return pl.pallas_call(
        functools.partial(_mm_kernel, scale=scale),
        grid=(n // tm, m // tn),
        in_specs=[pl.BlockSpec((tm, k), lambda i, j: (i, 0)),
                  pl.BlockSpec((k, tn), lambda i, j: (0, j))],
        out_specs=pl.BlockSpec((tm, tn), lambda i, j: (i, j)),
        out_shape=jax.ShapeDtypeStruct((n, m), out_dtype),
        compiler_params=_params("parallel", "parallel"),
        name="mm",
    )(x, w)


def _mm_t_kernel(w_ref, x_ref, *o_refs, scale):
    acc = _dot_nt(w_ref[...], x_ref[0].astype(BF16))
    if scale is not None:
        acc = acc * scale
    for o_ref in o_refs:
        o_ref[0] = acc.astype(o_ref.dtype)


def _mm_t(w_t, x, out_dtypes, row_start=0, rows=None, scale=None, tm=512, tn=512):
    b, t, k = x.shape
    m = w_t.shape[0]
    rows = t - row_start if rows is None else rows
    tm, tn = min(tm, rows), min(tn, m)
    off = row_start // tm
    return pl.pallas_call(
        functools.partial(_mm_t_kernel, scale=scale),
        grid=(b, rows // tm, m // tn),
        in_specs=[pl.BlockSpec((tn, k), lambda bi, i, j: (j, 0)),
                  pl.BlockSpec((1, tm, k), lambda bi, i, j: (bi, i + off, 0))],
        out_specs=[pl.BlockSpec((1, tn, tm), lambda bi, i, j: (bi, j, i)) for _ in out_dtypes],
        out_shape=[jax.ShapeDtypeStruct((b, m, rows), dt) for dt in out_dtypes],
        compiler_params=_params("parallel", "parallel", "parallel"),
        name="mm_t",
    )(w_t, x)


def _log_sigmoid(y):
    return jnp.minimum(y, 0.0) - jnp.log1p(jnp.exp(-jnp.abs(y)))


def _gate_kernel(x_ref, wft_ref, wf_ref, bfc_ref, bfr_ref, tri_ref, lf_ref, cum_ref, carry_ref):
    @pl.when(pl.program_id(1) == 0)
    def _():
        carry_ref[...] = jnp.zeros_like(carry_ref)

    xb = x_ref[0].astype(BF16)
    lf_ref[0] = _log_sigmoid(_dot_nt(wft_ref[...], xb) + bfc_ref[...])
    lf = _log_sigmoid(_dot(xb, wf_ref[...]) + bfr_ref[...])
    hi, mid, lo = _split3(lf)
    tri = tri_ref[...]
    cs = _dot(tri, hi) + _dot(tri, mid) + _dot(tri, lo) + carry_ref[0:1, :]
    cum_ref[0] = cs * LOG2E
    carry_ref[...] = jnp.broadcast_to(cs[cs.shape[0] - 1:, :], carry_ref.shape)


def _gate(x, wf_t, wf, b_f, tm=512):
    b, t, k = x.shape
    h = wf_t.shape[0]
    tm = min(tm, t)
    r = jnp.arange(tm)
    tri = (r[None, :] <= r[:, None]).astype(BF16)
    fixed = lambda shape: pl.BlockSpec(shape, lambda bi, i: (0, 0))
    return pl.pallas_call(
        _gate_kernel,
        grid=(b, t // tm),
        in_specs=[pl.BlockSpec((1, tm, k), lambda bi, i: (bi, i, 0)),
                  fixed((h, k)), fixed((k, h)), fixed((h, 1)), fixed((1, h)), fixed((tm, tm))],
        out_specs=[pl.BlockSpec((1, h, tm), lambda bi, i: (bi, 0, i)),
                   pl.BlockSpec((1, tm, h), lambda bi, i: (bi, i, 0))],
        out_shape=[jax.ShapeDtypeStruct((b, h, t), F32), jax.ShapeDtypeStruct((b, t, h), F32)],
        scratch_shapes=[pltpu.VMEM((8, h), F32)],
        compiler_params=_params("arbitrary", "arbitrary"),
        name="gate",
    )(x, wf_t, wf, b_f.reshape(h, 1), b_f.reshape(1, h), tri)


def _fox_attn_kernel(qt_ref, k_ref, vt_ref, c_ref, o_ref,
                     s0_ref, s1_ref, p0_ref, p1_ref, al0_ref, al1_ref, acc_ref, m_ref, *, tq):
    iq = pl.program_id(2)
    qt = qt_ref[0]
    row = lax.broadcasted_iota(jnp.int32, qt.shape, 0)
    zero = jnp.zeros_like(qt)
    qtm = [jnp.where(row < HEAD_DIM, qt, zero), jnp.where(row >= HEAD_DIM, qt, zero)]
    ones = jnp.ones((BF16_ROWS, tq), BF16)
    heads = range(HEADS_PER_SLAB)

    s_refs, p_refs, al_refs = (s0_ref, s1_ref), (p0_ref, p1_ref), (al0_ref, al1_ref)

    def scores(j, slot):
        start = pl.multiple_of(j * tq, tq)
        kc = k_ref[0, pl.ds(start, tq), :]
        for a in heads:
            s_refs[slot][a] = _dot(kc, qtm[a]) - c_ref[0, 0, pl.ds(start, tq), a:a + 1]

    def softmax(slot, masked):
        for a in heads:
            st = s_refs[slot][a]
            if masked:
                krow = lax.broadcasted_iota(jnp.int32, st.shape, 0)
                qcol = lax.broadcasted_iota(jnp.int32, st.shape, 1)
                st = jnp.where(krow <= qcol, st, NEG)
            m_old = m_ref[a]
            m_new = jnp.maximum(m_old, jnp.max(st, axis=0, keepdims=True))
            m_ref[a] = m_new
            al_refs[slot][a] = jnp.exp2(m_old - m_new)
            p_refs[slot][a] = jnp.exp2(st - m_new[0:1, :]).astype(BF16)

    def pv(j, slot):
        start = pl.multiple_of(jnp.maximum(j, 0) * tq, tq)
        for a in heads:
            vta = jnp.concatenate([vt_ref[0, a * HEAD_DIM:(a + 1) * HEAD_DIM, pl.ds(start, tq)], ones], axis=0)
            acc_ref[a] = al_refs[slot][a, 0:1, :] * acc_ref[a] + _dot(vta, p_refs[slot][a])

    def stage(j, slot):
        scores(j + 1, 1 - slot)
        softmax(slot, False)
        pv(j - 1, 1 - slot)

    scores(0, 0)
    acc_ref[...] = jnp.zeros_like(acc_ref)
    m_ref[...] = jnp.full_like(m_ref, NEG)
    al1_ref[...] = jnp.ones_like(al1_ref)
    p1_ref[...] = jnp.zeros_like(p1_ref)

    def body(jj, _):
        stage(2 * jj, 0)
        stage(2 * jj + 1, 1)
        return 0

    lax.fori_loop(0, iq // 2, body, 0)
    odd = iq % 2 == 1

    @pl.when(odd)
    def _():
        stage(iq - 1, 0)

    for slot in (0, 1):
        @pl.when(odd == (slot == 1))
        def _():
            softmax(slot, True)
            pv(iq - 1, 1 - slot)
            pv(iq, slot)

    outs = [acc_ref[a, :HEAD_DIM, :] / acc_ref[a, HEAD_DIM:HEAD_DIM + 1, :] for a in heads]
    o_ref[0] = jnp.concatenate(outs, axis=0).astype(o_ref.dtype)


def _fox_attn(q_t, k, v_t, cum_tok, tq=512):
    b, d, t = q_t.shape
    n_slab = d // LANES
    tq = min(tq, t)
    c4 = cum_tok.reshape(b, t, n_slab, HEADS_PER_SLAB).transpose(0, 2, 1, 3)
    return pl.pallas_call(
        functools.partial(_fox_attn_kernel, tq=tq),
        grid=(b, n_slab, t // tq),
        in_specs=[pl.BlockSpec((1, LANES, tq), lambda bi, p, i: (bi, p, i)),
                  pl.BlockSpec((1, t, LANES), lambda bi, p, i: (bi, 0, p)),
                  pl.BlockSpec((1, LANES, t), lambda bi, p, i: (bi, p, 0)),
                  pl.BlockSpec((1, 1, t, HEADS_PER_SLAB), lambda bi, p, i: (bi, p, 0, 0))],
        out_specs=pl.BlockSpec((1, LANES, tq), lambda bi, p, i: (bi, p, i)),
        out_shape=jax.ShapeDtypeStruct((b, d, t), BF16),
        scratch_shapes=([pltpu.VMEM((HEADS_PER_SLAB, tq, tq), F32)] * 2
                        + [pltpu.VMEM((HEADS_PER_SLAB, tq, tq), BF16)] * 2
                        + [pltpu.VMEM((HEADS_PER_SLAB, 8, tq), F32)] * 2
                        + [pltpu.VMEM((HEADS_PER_SLAB, HEAD_DIM + BF16_ROWS, tq), F32),
                           pltpu.VMEM((HEADS_PER_SLAB, 8, tq), F32)]),
        compiler_params=_params("parallel", "parallel", "arbitrary"),
        name="fox_attn",
    )(q_t, k, v_t, c4)


def _proj_norm_t_kernel(at_ref, w_ref, x_ref, g_ref, b_ref, o_ref, *, alpha):
    mix = lax.dot_general(at_ref[0], w_ref[...], (((0,), (0,)), ((), ())), preferred_element_type=F32)
    o_ref[0] = _layer_norm(alpha * x_ref[0] + mix, g_ref[...], b_ref[...])


def _proj_norm_t(a_t, w, x, g, b, alpha, tm=512):
    nb, ka, t = a_t.shape
    d = w.shape[1]
    tm = min(tm, t)
    fixed = lambda shape: pl.BlockSpec(shape, lambda bi, i: (0, 0))
    return pl.pallas_call(
        functools.partial(_proj_norm_t_kernel, alpha=alpha),
        grid=(nb, t // tm),
        in_specs=[pl.BlockSpec((1, ka, tm), lambda bi, i: (bi, 0, i)),
                  fixed((ka, d)),
                  pl.BlockSpec((1, tm, d), lambda bi, i: (bi, i, 0)),
                  fixed((1, d)), fixed((1, d))],
        out_specs=pl.BlockSpec((1, tm, d), lambda bi, i: (bi, i, 0)),
        out_shape=jax.ShapeDtypeStruct((nb, t, d), F32),
        compiler_params=_params("parallel", "parallel"),
        name="proj_norm_t",
    )(a_t, w, x, g, b)


def _route(logits):
    lane = lax.broadcasted_iota(jnp.int32, logits.shape, 1).astype(F32)
    far = float(LANES)
    gl = jnp.where(lane < N_EXPERT_GROUPS, logits, NEG)
    gmax = jnp.max(gl, axis=1, keepdims=True)
    gsel = jnp.min(jnp.where(gl == gmax, lane, far), axis=1, keepdims=True)
    psel = 1.0 / jnp.sum(jnp.exp(gl - gmax), axis=1, keepdims=True)
    lo = N_EXPERT_GROUPS + gsel * EXPERTS_PER_GROUP
    el = jnp.where((lane >= lo) & (lane < lo + EXPERTS_PER_GROUP), logits, NEG)
    v1 = jnp.max(el, axis=1, keepdims=True)
    i1 = jnp.min(jnp.where(el == v1, lane, far), axis=1, keepdims=True)
    el2 = jnp.where(lane == i1, NEG, el)
    v2 = jnp.max(el2, axis=1, keepdims=True)
    i2 = jnp.min(jnp.where(el2 == v2, lane, far), axis=1, keepdims=True)
    e2 = jnp.exp(v2 - v1)
    den = 1.0 + e2
    return jnp.where(lane == i1, psel / den, jnp.where(lane == i2, psel * e2 / den, 0.0))


def _moe_kernel(x_ref, wr_ref, br_ref, wg_ref, wu_ref, wd_ref, g_ref, b_ref, o_ref,
                xb_ref, gates_ref, acc_ref, *, alpha):
    e = pl.program_id(1)

    @pl.when(e == 0)
    def _():
        x = x_ref[...]
        xh = x.astype(BF16)
        xl = (x - xh.astype(F32)).astype(BF16)
        wr = wr_ref[...]
        wh = wr.astype(BF16)
        wl = (wr - wh.astype(F32)).astype(BF16)
        xb_ref[...] = xh
        logits = _dot(xh, wh) + _dot(xl, wh) + _dot(xh, wl) + br_ref[...]
        gates_ref[...] = _route(logits)
        acc_ref[...] = jnp.zeros_like(acc_ref)

    xb = xb_ref[...]
    hg = _dot(xb, wg_ref[0])
    hu = _dot(xb, wu_ref[0])
    gcol = _lane_col(gates_ref[...], e + N_EXPERT_GROUPS)
    h = hg * (1.0 / (1.0 + jnp.exp(-hg))) * hu
    acc_ref[...] += _dot((h * gcol).astype(BF16), wd_ref[0])

    @pl.when(e == pl.num_programs(1) - 1)
    def _():
        o_ref[...] = _layer_norm(alpha * x_ref[...] + acc_ref[...], g_ref[...], b_ref[...])


def _moe_norm(x, wr, br, wg, wu, wd, g, b, alpha, tm=1024):
    n, d = x.shape
    ne, _, f = wg.shape
    tm = min(tm, n)
    return pl.pallas_call(
        functools.partial(_moe_kernel, alpha=alpha),
        grid=(n // tm, ne),
        in_specs=[pl.BlockSpec((tm, d), lambda i, e: (i, 0)),
                  pl.BlockSpec((d, LANES), lambda i, e: (0, 0)),
                  pl.BlockSpec((1, LANES), lambda i, e: (0, 0)),
                  pl.BlockSpec((1, d, f), lambda i, e: (e, 0, 0)),
                  pl.BlockSpec((1, d, f), lambda i, e: (e, 0, 0)),
                  pl.BlockSpec((1, f, d), lambda i, e: (e, 0, 0)),
                  pl.BlockSpec((1, d), lambda i, e: (0, 0)),
                  pl.BlockSpec((1, d), lambda i, e: (0, 0))],
        out_specs=pl.BlockSpec((tm, d), lambda i, e: (i, 0)),
        out_shape=jax.ShapeDtypeStruct((n, d), F32),
        scratch_shapes=[pltpu.VMEM((tm, d), BF16), pltpu.VMEM((tm, LANES), F32), pltpu.VMEM((tm, d), F32)],
        compiler_params=_params("parallel", "arbitrary"),
        name="moe_norm",
    )(x, wr, br, wg, wu, wd, g, b)


def _alibi_slope(g, h, n_heads):
    return 2.0 ** (-8.0 * (h * N_DIL_GROUPS + g + 1) / (N_DIL_GROUPS * n_heads))


def _mm_dil_kernel(x_ref, w_ref, o_ref, acc_ref, *, rate):
    acc = _dot(x_ref[0].astype(BF16), w_ref[...])
    if rate == 1:
        o_ref[0, 0] = acc.astype(o_ref.dtype)
        return
    rows = acc.shape[0] // rate
    for c in range(acc_ref.shape[0]):
        acc_ref[c] = acc[:, c * LANES:(c + 1) * LANES]
    for r in range(rate):
        for c in range(acc_ref.shape[0]):
            o_ref[0, r, :, c * LANES:(c + 1) * LANES] = \
                acc_ref[c, pl.ds(r, rows, stride=rate), :].astype(o_ref.dtype)


def _mm_dil(x, w, rate, tm=1024, tn=512):
    b, t, k = x.shape
    m = w.shape[1]
    tm, tn = min(tm, t), min(tn, m)
    return pl.pallas_call(
        functools.partial(_mm_dil_kernel, rate=rate),
        grid=(b, t // tm, m // tn),
        in_specs=[pl.BlockSpec((1, tm, k), lambda bi, i, j: (bi, i, 0)),
                  pl.BlockSpec((k, tn), lambda bi, i, j: (0, j))],
        out_specs=pl.BlockSpec((1, rate, tm // rate, tn), lambda bi, i, j: (bi, 0, i, j)),
        out_shape=jax.ShapeDtypeStruct((b, rate, t // rate, m), BF16),
        scratch_shapes=[pltpu.VMEM((tn // LANES, tm, LANES), F32)],
        compiler_params=_params("parallel", "parallel", "parallel"),
        name=f"mm_dil{rate}",
    )(x, w)


def _dil_attn_kernel(q_ref, kc_ref, kp_ref, vc_ref, vp_ref, o_ref, lse_ref, kbuf, vbuf, *, tu, rate, slopes):
    i = pl.program_id(2)
    kbuf[0:DIL_KEYS] = kp_ref[0, 0]
    kbuf[DIL_KEYS:] = kc_ref[0, 0]
    vbuf[0:DIL_KEYS] = vp_ref[0, 0]
    vbuf[DIL_KEYS:] = vc_ref[0, 0]
    shape = (DIL_KEYS, 2 * DIL_KEYS)
    a = lax.broadcasted_iota(jnp.int32, shape, 0)
    c = lax.broadcasted_iota(jnp.int32, shape, 1)
    dist = a - c + DIL_KEYS
    base = jnp.where((dist >= 0) & (dist <= DIL_KEYS), (-rate * dist).astype(F32), NEG)
    base_first = jnp.where(c >= DIL_KEYS, base, NEG)
    lane = lax.broadcasted_iota(jnp.int32, (DIL_KEYS, LANES), 1)
    left = lane < HEAD_DIM
    for s in range(tu // DIL_KEYS):
        bias = jnp.where(i == 0, base_first, base) if s == 0 else base
        r0 = s * DIL_KEYS
        for p in range(q_ref.shape[3] // LANES):
            cs = slice(p * LANES, (p + 1) * LANES)
            qp = q_ref[0, 0, r0:r0 + DIL_KEYS, cs]
            kk = kbuf[r0:r0 + 2 * DIL_KEYS, cs]
            vv = vbuf[r0:r0 + 2 * DIL_KEYS, cs]
            outs, lses = [], []
            for hh in range(HEADS_PER_SLAB):
                qm = jnp.where(left if hh == 0 else jnp.logical_not(left), qp, jnp.zeros_like(qp))
                sc = _dot_nt(qm, kk) + slopes[HEADS_PER_SLAB * p + hh] * bias
                m = jnp.max(sc, axis=1, keepdims=True)
                e = jnp.exp(sc - m)
                l = jnp.sum(e, axis=1, keepdims=True)
                outs.append(_dot(e.astype(BF16), vv) / l)
                lses.append(m + jnp.log(l))
            o_ref[0, 0, r0:r0 + DIL_KEYS, cs] = jnp.where(left, outs[0], outs[1])
            lse_ref[0, 0, r0:r0 + DIL_KEYS, cs] = jnp.where(left, lses[0], lses[1])


def _dil_attn(qkv, g, tu=512):
    b, rate, u, dg3 = qkv.shape
    dg = dg3 // 3
    tu = min(tu, u)
    ratio = tu // DIL_KEYS
    slopes = tuple(_alibi_slope(g, h, dg // HEAD_DIM) for h in range(dg // HEAD_DIM))
    cur = lambda col: pl.BlockSpec((1, 1, tu, dg), lambda bi, r, i: (bi, r, i, col))
    prev = lambda col: pl.BlockSpec((1, 1, DIL_KEYS, dg),
                                    lambda bi, r, i: (bi, r, jnp.maximum(i * ratio - 1, 0), col))
    return pl.pallas_call(
        functools.partial(_dil_attn_kernel, tu=tu, rate=rate, slopes=slopes),
        grid=(b, rate, u // tu),
        in_specs=[cur(0), cur(1), prev(1), cur(2), prev(2)],
        out_specs=[cur(0)] * 2,
        out_shape=[jax.ShapeDtypeStruct((b, rate, u, dg), F32)] * 2,
        scratch_shapes=[pltpu.VMEM((tu + DIL_KEYS, dg), BF16)] * 2,
        compiler_params=_params("parallel", "parallel", "arbitrary"),
        name=f"dil_attn{g}",
    )(qkv, qkv, qkv, qkv, qkv)


def _mix_norm_kernel(*refs, alpha, rates):
    ng = len(rates)
    o_refs, l_refs = refs[:ng], refs[ng:2 * ng]
    w_ref, x_ref, g_ref, b_ref, out_ref = refs[2 * ng:2 * ng + 5]
    bufs = refs[2 * ng + 5:]

    def natural(ref, buf, rate):
        if rate == 1:
            return ref[0, 0]
        for r in range(rate):
            for c in range(buf.shape[0]):
                buf[c, pl.ds(r, ref.shape[2], stride=rate), :] = ref[0, r, :, c * LANES:(c + 1) * LANES]
        return jnp.concatenate([buf[c] for c in range(buf.shape[0])], axis=1)

    os_ = [natural(o_refs[g], bufs[g], rates[g]) for g in range(ng)]
    ls = [natural(l_refs[g], bufs[ng + g], rates[g]) for g in range(ng)]
    m = functools.reduce(jnp.maximum, ls)
    es = [jnp.exp(l - m) for l in ls]
    o = sum(e * o_ for e, o_ in zip(es, os_)) / sum(es)
    mix = _dot(o.astype(BF16), w_ref[...])
    out_ref[0] = _layer_norm(alpha * x_ref[0] + mix, g_ref[...], b_ref[...])


def _mix_norm(os_, ls, w, x, g, b, alpha, tm=512):
    nb, t, d = x.shape
    dg = w.shape[0]
    tm = min(tm, t)
    rates = tuple(o.shape[1] for o in os_)
    grp = lambda rate: pl.BlockSpec((1, rate, tm // rate, dg), lambda bi, i: (bi, 0, i, 0))
    fixed = lambda shape: pl.BlockSpec(shape, lambda bi, i: (0, 0))
    xrow = pl.BlockSpec((1, tm, d), lambda bi, i: (bi, i, 0))
    return pl.pallas_call(
        functools.partial(_mix_norm_kernel, alpha=alpha, rates=rates),
        grid=(nb, t // tm),
        in_specs=[grp(r) for r in rates] * 2 + [fixed((dg, d)), xrow, fixed((1, d)), fixed((1, d))],
        out_specs=xrow,
        out_shape=jax.ShapeDtypeStruct((nb, t, d), F32),
        scratch_shapes=[pltpu.VMEM((dg // LANES, tm, LANES), F32)] * (2 * len(rates)),
        compiler_params=_params("parallel", "parallel"),
        name="mix_norm",
    )(*os_, *ls, w, x, g, b)


def _fox_decode_kernel(pt_ref, qt_ref, u_ref, *refs, pages):
    k_refs, v_refs, lf_refs = refs[:pages], refs[pages:2 * pages], refs[2 * pages:3 * pages]
    ot_ref, mt_ref, lt_ref, ct_ref, qcol_ref, acc_ref, m_ref, l_ref, c_ref = refs[3 * pages:]
    b, j = pl.program_id(0), pl.program_id(1)
    nh = m_ref.shape[0]

    @pl.when((b == 0) & (j == 0))
    def _():
        for r in (ot_ref, mt_ref, lt_ref, ct_ref):
            r[...] = jnp.zeros_like(r)

    @pl.when(j == 0)
    def _():
        qcol = _lane_col(qt_ref[...], b) * ATTN_SCALE
        qcol_ref[...] = jnp.broadcast_to(qcol.astype(BF16).astype(F32), qcol_ref.shape)
        acc_ref[...] = jnp.zeros_like(acc_ref)
        m_ref[...] = jnp.full_like(m_ref, NEG)
        l_ref[...] = jnp.zeros_like(l_ref)
        c_ref[...] = jnp.zeros_like(c_ref)

    u = u_ref[...]
    for pi in range(pages):
        hi, mid, lo = _split3(lf_refs[pi][0])
        cs = _dot(hi, u) + _dot(mid, u) + _dot(lo, u) + c_ref[...]
        c_ref[...] = jnp.broadcast_to(cs[:, LANES - 1:LANES], c_ref.shape)
        s = _seg_sum(qcol_ref[...] * k_refs[pi][0], HEAD_DIM) - cs
        m_old = m_ref[...]
        m_new = jnp.maximum(m_old, jnp.max(s, axis=1, keepdims=True))
        alpha = jnp.exp(m_old - m_new)
        p = jnp.exp(s - m_new)
        m_ref[...] = m_new
        l_ref[...] = alpha * l_ref[...] + p
        acc_ref[...] = (acc_ref[...] * _seg_expand(alpha, HEAD_DIM)
                        + _seg_expand(p, HEAD_DIM) * v_refs[pi][0])

    @pl.when(j == pl.num_programs(1) - 1)
    def _():
        ot_ref[...] = _put_lane(ot_ref[...], jnp.sum(acc_ref[...], axis=1, keepdims=True), b)
        lt_ref[...] = _put_lane(lt_ref[...], jnp.sum(l_ref[...], axis=1, keepdims=True), b)
        mt_ref[...] = _put_lane(mt_ref[...], m_ref[:, 0:1], b)
        ct_ref[...] = _put_lane(ct_ref[...], c_ref[:, 0:1], b)


def _fox_decode(page_table, q_t, k_pages, v_pages, lf_pages, pages=8):
    d, nb = q_t.shape
    nh = lf_pages.shape[1]
    n_pages = page_table.shape[1]
    ps = k_pages.shape[2]
    pt = page_table.reshape(-1)

    def page_spec(rows, pi):
        return pl.BlockSpec((1, rows, ps), lambda b, j, pt_ref: (pt_ref[b * n_pages + j * pages + pi], 0, 0))

    fixed = lambda shape: pl.BlockSpec(shape, lambda b, j, pt_ref: (0, 0))
    grid_spec = pltpu.PrefetchScalarGridSpec(
        num_scalar_prefetch=1,
        grid=(nb, n_pages // pages),
        in_specs=([fixed((d, nb)), fixed((LANES, LANES))]
                  + [page_spec(d, pi) for pi in range(pages)] * 2
                  + [page_spec(nh, pi) for pi in range(pages)]),
        out_specs=[fixed((d, nb)), fixed((nh, nb)), fixed((nh, nb)), fixed((nh, nb))],
        scratch_shapes=[pltpu.VMEM((d, ps), F32), pltpu.VMEM((d, ps), F32),
                        pltpu.VMEM((nh, ps), F32), pltpu.VMEM((nh, ps), F32), pltpu.VMEM((nh, ps), F32)],
    )
    return pl.pallas_call(
        functools.partial(_fox_decode_kernel, pages=pages),
        grid_spec=grid_spec,
        out_shape=[jax.ShapeDtypeStruct((d, nb), F32)] + [jax.ShapeDtypeStruct((nh, nb), F32)] * 3,
        compiler_params=_params("arbitrary", "arbitrary"),
        name="fox_decode",
    )(pt, q_t, _upper_tri(), *([k_pages] * pages), *([v_pages] * pages), *([lf_pages] * pages))


def _fox_merge_kernel(ot_ref, mt_ref, lt_ref, ct_ref, qt_ref, kn_ref, vn_ref, lfn_ref,
                      w_ref, x_ref, g_ref, b_ref, out_ref, *, alpha):
    qs = (qt_ref[...] * ATTN_SCALE).astype(BF16).astype(F32)
    s_new = _seg_sum(qs * kn_ref[...], HEAD_DIM) - (ct_ref[...] + lfn_ref[...])
    m_p = mt_ref[...]
    m = jnp.maximum(m_p, s_new)
    a_p = jnp.exp(m_p - m)
    a_n = jnp.exp(s_new - m)
    l = lt_ref[...] * a_p + a_n
    o_t = (ot_ref[...] * _seg_expand(a_p, HEAD_DIM) + _seg_expand(a_n, HEAD_DIM) * vn_ref[...]) \
        / _seg_expand(l, HEAD_DIM)
    mix = _dot(o_t.T.astype(BF16), w_ref[...])
    out_ref[...] = _layer_norm(alpha * x_ref[...] + mix, g_ref[...], b_ref[...])


def _fox_merge(o_t, m_t, l_t, c_t, q_t, kn_t, vn_t, lfn_t, w, x, g, b, alpha):
    return pl.pallas_call(
        functools.partial(_fox_merge_kernel, alpha=alpha),
        out_shape=jax.ShapeDtypeStruct(x.shape, F32),
        compiler_params=pltpu.CompilerParams(vmem_limit_bytes=VMEM_LIMIT),
        name="fox_merge",
    )(o_t, m_t, l_t, c_t, q_t, kn_t, vn_t, lfn_t, w, x, g, b)


def _dil_decode_kernel(buf_ref, qt_ref, kvn_ref, sl_ref, out_ref, ot_ref, mt_ref, lt_ref, p_ref,
                       *, window, rate):
    bi, c = pl.program_id(0), pl.program_id(1)

    @pl.when((bi == 0) & (c == 0))
    def _():
        for r in (ot_ref, mt_ref, lt_ref):
            r[...] = jnp.zeros_like(r)

    t = lax.broadcasted_iota(jnp.int32, (1, window), 1)
    for bb in range(buf_ref.shape[0]):
        b = bi * buf_ref.shape[0] + bb
        x = buf_ref[bb]

        @pl.when(c == 0)
        def _():
            qcol = (_lane_col(qt_ref[...], b) * ATTN_SCALE).astype(BF16).astype(F32)
            s = _seg_sum(x * qcol, HEAD_DIM)
            dist = (window - t).astype(F32)
            s = jnp.where((t & (rate - 1)) == 0, s - sl_ref[:, 0:1] * dist, NEG)
            m = jnp.max(s, axis=1, keepdims=True)
            p = jnp.exp(s - m)
            p_ref[bb] = p
            mt_ref[...] = _put_lane(mt_ref[...], m, b)
            lt_ref[...] = _put_lane(lt_ref[...], jnp.sum(p, axis=1, keepdims=True), b)

        @pl.when(c == 1)
        def _():
            o = jnp.sum(x * _seg_expand(p_ref[bb], HEAD_DIM), axis=1, keepdims=True)
            ot_ref[...] = _put_lane(ot_ref[...], o, b)

        newcol = _lane_col(kvn_ref[...], b)
        shifted = pltpu.roll(x, window - 1, axis=1)
        out_ref[bb] = jnp.where(t == window - 1, newcol, shifted)


def _dil_decode(buf_t, qg_t, kvn_t, g):
    nb, _, window = buf_t.shape
    dg = qg_t.shape[0]
    nh = dg // HEAD_DIM
    rate = DIL_RATES[g]
    assert rate & (rate - 1) == 0 and window == DIL_KEYS * rate
    slopes = jnp.asarray([_alibi_slope(g, h, nh) for h in range(nh)], F32)
    sl = jnp.broadcast_to(slopes.reshape(nh, 1), (nh, LANES))
    fixed = lambda shape: pl.BlockSpec(shape, lambda b, c: (0, 0))
    per_step = max(1, min(nb, DECODE_BLOCK_BYTES // (dg * window * 4)))
    assert nb % per_step == 0
    return pl.pallas_call(
        functools.partial(_dil_decode_kernel, window=window, rate=rate),
        grid=(nb // per_step, 2),
        in_specs=[pl.BlockSpec((per_step, dg, window), lambda b, c: (b, c, 0)),
                  fixed((dg, nb)),
                  pl.BlockSpec((dg, nb), lambda b, c: (c, 0)),
                  fixed((nh, LANES))],
        out_specs=[pl.BlockSpec((per_step, dg, window), lambda b, c: (b, c, 0)),
                   fixed((dg, nb)), fixed((nh, nb)), fixed((nh, nb))],
        out_shape=[jax.ShapeDtypeStruct(buf_t.shape, F32),
                   jax.ShapeDtypeStruct((dg, nb), F32),
                   jax.ShapeDtypeStruct((nh, nb), F32),
                   jax.ShapeDtypeStruct((nh, nb), F32)],
        scratch_shapes=[pltpu.VMEM((per_step, nh, window), F32)],
        compiler_params=_params("arbitrary", "arbitrary"),
        name=f"dil_decode{g}",
    )(buf_t, qg_t, kvn_t, sl)


def _dil_merge_kernel(*refs, alpha):
    states = refs[:3 * N_DIL_GROUPS]
    qt_ref, kvn_ref, w_ref, x_ref, g_ref, b_ref, out_ref = refs[3 * N_DIL_GROUPS:]
    dg = w_ref.shape[0]
    outs, lses = [], []
    for g in range(N_DIL_GROUPS):
        ot_ref, mt_ref, lt_ref = states[3 * g:3 * g + 3]
        qs =(qt_ref[g * dg:(g + 1) * dg, :] * ATTN_SCALE).astype(BF16).astype(F32)
        kn = kvn_ref[2 * g * dg:(2 * g + 1) * dg, :]
        vn = kvn_ref[(2 * g + 1) * dg:(2 * g + 2) * dg, :]
        s_new = _seg_sum(qs * kn, HEAD_DIM)
        m_p = mt_ref[...]
        m = jnp.maximum(m_p, s_new)
        a_p = jnp.exp(m_p - m)
        a_n = jnp.exp(s_new - m)
        l = lt_ref[...] * a_p + a_n
        o = (ot_ref[...] * _seg_expand(a_p, HEAD_DIM) + _seg_expand(a_n, HEAD_DIM) * vn) \
            / _seg_expand(l, HEAD_DIM)
        outs.append(o)
        lses.append(m + jnp.log(l))
    m = jnp.maximum(jnp.maximum(lses[0], lses[1]), lses[2])
    es = [jnp.exp(l - m) for l in lses]
    tot = es[0] + es[1] + es[2]
    o_t = sum(_seg_expand(e / tot, HEAD_DIM) * o for e, o in zip(es, outs))
    mix = _dot(o_t.T.astype(BF16), w_ref[...])
    out_ref[...] = _layer_norm(alpha * x_ref[...] + mix, g_ref[...], b_ref[...])


def _dil_merge(states, q_t, kvn_t, w, x, g, b, alpha):
    flat = [a for st in states for a in st]
    return pl.pallas_call(
        functools.partial(_dil_merge_kernel, alpha=alpha),
        out_shape=jax.ShapeDtypeStruct(x.shape, F32),
        compiler_params=pltpu.CompilerParams(vmem_limit_bytes=VMEM_LIMIT),
        name="dil_merge",
    )(*flat, q_t, kvn_t, w, x, g, b)


def kernel(x_prompt, x_sample, cache_fox_k, cache_fox_v, cache_fox_logf, cache_dil_kv_0, cache_dil_kv_1,
           cache_dil_kv_2, page_table, w_qkvf_a, b_f_a, w_o_a, w_q_b, w_kv_b, w_o_b, ln_g, ln_b,
           w_router_grp, b_router_grp, w_router_exp, b_router_exp, w_gate, w_up, w_down):
    nb, t, d = x_prompt.shape
    ns = x_sample.shape[0]
    depth = w_gate.shape[0]
    alpha = (2 * depth) ** 0.25
    nh = d // HEAD_DIM
    dg = w_o_b.shape[1]
    dq = N_DIL_GROUPS * dg

    w_a = w_qkvf_a[0]
    wk = w_a[:, d:2 * d].astype(BF16)
    wqkv_t = w_a[:, :3 * d].T.astype(BF16)
    wf = w_a[:, 3 * d:].astype(BF16)
    wf_t = wf.T
    b_f = b_f_a[0]
    wo_a = w_o_a[0].astype(BF16)
    wqkv_b_t = jnp.concatenate([w_q_b[0], w_kv_b], axis=1).T.astype(BF16)
    wkv_t = w_kv_b.T.astype(BF16)
    wo_b = w_o_b[0].astype(BF16)
    wqkv_g = [jnp.concatenate([w_q_b[0][:, g * dg:(g + 1) * dg] * ATTN_SCALE,
                               w_kv_b[:, 2 * g * dg:(2 * g + 2) * dg]], axis=1).astype(BF16)
              for g in range(N_DIL_GROUPS)]
    pad = LANES - N_EXPERT_GROUPS - N_EXPERTS
    wr = [jnp.pad(jnp.concatenate([w_router_grp[l], w_router_exp[l]], axis=1), ((0, 0), (0, pad)))
          for l in range(depth)]
    br = [jnp.pad(jnp.concatenate([b_router_grp[l], b_router_exp[l]]), (0, pad)).reshape(1, LANES)
          for l in range(depth)]
    wg = [w_gate[l].astype(BF16) for l in range(depth)]
    wu = [w_up[l].astype(BF16) for l in range(depth)]
    wd = [w_down[l].astype(BF16) for l in range(depth)]
    lng = lambda l, i: ln_g[l, i].reshape(1, d)
    lnb = lambda l, i: ln_b[l, i].reshape(1, d)

    def moe(x, l):
        return _moe_norm(x, wr[l], br[l], wg[l], wu[l], wd[l], lng(l, 1), lnb(l, 1), alpha)

    xp = x_prompt.reshape(nb * t, d)
    (q_t,) = _mm_t(wqkv_t[:d], x_prompt, (BF16,), scale=ATTN_SCALE * LOG2E)
    k_rm = _mm(xp, wk, BF16).reshape(nb, t, d)
    (k_t,) = _mm_t(wqkv_t[d:2 * d], x_prompt, (F32,))
    v_t, v_tb = _mm_t(wqkv_t[2 * d:], x_prompt, (F32, BF16))
    lf_t, cum_tok = _gate(x_prompt, wf_t, wf, b_f)
    o_t = _fox_attn(q_t, k_rm, v_tb, cum_tok)
    x1 = _proj_norm_t(o_t, wo_a, x_prompt, lng(0, 0), lnb(0, 0), alpha)
    x2 = moe(x1.reshape(nb * t, d), 0).reshape(nb, t, d)

    tail = min(max(DIL_WINDOWS), t)
    (kv_tail_t,) = _mm_t(wkv_t, x2, (F32,), row_start=t - tail, rows=tail)
    os_, ls_ = zip(*[_dil_attn(_mm_dil(x2, wqkv_g[g], DIL_RATES[g]), g) for g in range(N_DIL_GROUPS)])
    x3 = _mix_norm(os_, ls_, wo_b, x2, lng(1, 0), lnb(1, 0), alpha)
    y_prompt = moe(x3.reshape(nb * t, d), 1).reshape(nb, t, d)

    def feat_major_out(a_t, lead):
        b_, f_, t_ = a_t.shape
        return a_t.reshape(*lead, b_, f_ // HEAD_DIM, HEAD_DIM, t_).transpose(
            *range(len(lead)), len(lead), len(lead) + 3, len(lead) + 1, len(lead) + 2)

    fox_k_p = feat_major_out(k_t, (1,))
    fox_v_p = feat_major_out(v_t, (1,))
    fox_lf_p = lf_t.transpose(0, 2, 1)[None]
    dil_p = []
    for g, w in enumerate(DIL_WINDOWS):
        wlen = min(w, t)
        sl = kv_tail_t[:, 2 * g * dg:(2 * g + 2) * dg, tail - wlen:]
        dil_p.append(sl.reshape(nb, 2, dg // HEAD_DIM, HEAD_DIM, wlen).transpose(0, 4, 1, 2, 3))

    xs = x_sample.reshape(ns, d)
    (qkv_s,) = _mm_t(wqkv_t, xs[None], (F32,))
    q_st, kn_t, vn_t = qkv_s[0, :d], qkv_s[0, d:2 * d], qkv_s[0, 2 * d:]
    lfn_t, _ = _gate(xs[None], wf_t, wf, b_f)
    lfn_t = lfn_t[0]
    n_phys, ps = cache_fox_k.shape[1], cache_fox_k.shape[2]
    k_pages = cache_fox_k[0].transpose(0, 2, 3, 1).reshape(n_phys, d, ps)
    v_pages = cache_fox_v[0].transpose(0, 2, 3, 1).reshape(n_phys, d, ps)
    lf_pages = cache_fox_logf[0].transpose(0, 2, 1)
    o_t, m_t, l_t, c_t = _fox_decode(page_table, q_st, k_pages, v_pages, lf_pages)
    x1s = _fox_merge(o_t, m_t, l_t, c_t, q_st, kn_t, vn_t, lfn_t, wo_a, xs, lng(0, 0), lnb(0, 0), alpha)
    x2s = moe(x1s, 0)

    (qkv_b,) = _mm_t(wqkv_b_t, x2s[None], (F32,))
    qb_t, kvn_t = qkv_b[0, :dq], qkv_b[0, dq:]
    states, dil_s = [], []
    for g, cache in enumerate((cache_dil_kv_0, cache_dil_kv_1, cache_dil_kv_2)):
        wlen = cache.shape[1]
        buf_t = cache.transpose(0, 2, 3, 4, 1).reshape(ns, 2 * dg, wlen)
        new_buf, og, mg, lg = _dil_decode(buf_t, qb_t[g * dg:(g + 1) * dg], kvn_t[2 * g * dg:(2 * g + 2) * dg], g)
        states.append((og, mg, lg))
        dil_s.append(new_buf.reshape(ns, 2, dg // HEAD_DIM, HEAD_DIM, wlen).transpose(0, 4, 1, 2, 3))
    x3s = _dil_merge(states, qb_t, kvn_t, wo_b, x2s, lng(1, 0), lnb(1, 0), alpha)
    y_sample = moe(x3s, 1).reshape(ns, 1, d)

    fox_k_s = kn_t.reshape(1, 1, nh, HEAD_DIM, ns).transpose(0, 4, 1, 2, 3)
    fox_v_s = vn_t.reshape(1, 1, nh, HEAD_DIM, ns).transpose(0, 4, 1, 2, 3)
    fox_lf_s = lfn_t.reshape(1, 1, nh, ns).transpose(0, 3, 1, 2)

    return (y_prompt, y_sample, fox_k_p, fox_v_p, fox_lf_p, dil_p[0], dil_p[1], dil_p[2],
            fox_k_s, fox_v_s, fox_lf_s, dil_s[0], dil_s[1], dil_s[2])
```

```python
import functools

import jax
import jax.numpy as jnp
from jax import lax
from jax.experimental import pallas as pl
from jax.experimental.pallas import tpu as pltpu

F32 = jnp.float32
BF16 = jnp.bfloat16

HEAD_DIM = 64
LANES = 128
HEADS_PER_SLAB = LANES // HEAD_DIM
DIL_WINDOWS = (128, 512, 2048)
DIL_RATES = (1, 4, 16)
N_DIL_GROUPS = len(DIL_WINDOWS)
DIL_KEYS = 128
N_EXPERT_GROUPS = 4
EXPERTS_PER_GROUP = 8
N_EXPERTS = N_EXPERT_GROUPS * EXPERTS_PER_GROUP
LN_EPS = 1e-5
ATTN_SCALE = HEAD_DIM ** -0.5
LOG2E = 1.4426950408889634
BF16_ROWS = 16
NEG = -1e30
VMEM_LIMIT = 48 * 1024 * 1024
DECODE_BLOCK_BYTES = 4 * 1024 * 1024


def _params(*sem):
    return pltpu.CompilerParams(dimension_semantics=sem, vmem_limit_bytes=VMEM_LIMIT)


def _dot(a, b):
    return jnp.dot(a, b, preferred_element_type=F32)


def _dot_nt(a, b):
    return lax.dot_general(a, b, (((1,), (1,)), ((), ())), preferred_element_type=F32)


def _split3(x):
    hi = x.astype(BF16)
    r1 = x - hi.astype(F32)
    mid = r1.astype(BF16)
    lo = (r1 - mid.astype(F32)).astype(BF16)
    return hi, mid, lo


def _layer_norm(z, g, b):
    mu = jnp.mean(z, axis=-1, keepdims=True)
    zc = z - mu
    var = jnp.mean(zc * zc, axis=-1, keepdims=True)
    return zc * lax.rsqrt(var + LN_EPS) * g + b


def _seg_sum(x, seg):
    return jnp.sum(x.reshape(x.shape[0] // seg, seg, x.shape[1]), axis=1)


def _seg_expand(x, seg):
    h, l = x.shape
    return jnp.broadcast_to(x[:, None, :], (h, seg, l)).reshape(h * seg, l)


def _lane_col(x, idx):
    lane = lax.broadcasted_iota(jnp.int32, x.shape, 1)
    return jnp.sum(jnp.where(lane == idx, x, 0.0), axis=1, keepdims=True)


def _put_lane(ref_val, col, idx):
    lane = lax.broadcasted_iota(jnp.int32, ref_val.shape, 1)
    return jnp.where(lane == idx, col, ref_val)


def _upper_tri():
    r = jnp.arange(LANES)
    return (r[:, None] <= r[None, :]).astype(BF16)


def _mm_kernel(x_ref, w_ref, o_ref, *, scale):
    acc = _dot(x_ref[...].astype(BF16), w_ref[...])
    if scale is not None:
        acc = acc * scale
    o_ref[...] = acc.astype(o_ref.dtype)


def _mm(x, w, out_dtype, scale=None, tm=1024, tn=512):
    n, k = x.shape
    m = w.shape[1]
    tm, tn = min(tm, n), min(tn, m)
    return pl.pallas_call(
        functools.partial(_mm_kernel, scale=scale),
        grid=(n // tm, m // tn),
        in_specs=[pl.BlockSpec((tm, k), lambda i, j: (i, 0)),
                  pl.BlockSpec((k, tn), lambda i, j: (0, j))],
        out_specs=pl.BlockSpec((tm, tn), lambda i, j: (i, j)),
        out_shape=jax.ShapeDtypeStruct((n, m), out_dtype),
        compiler_params=_params("parallel", "parallel"),
        name="mm",
    )(x, w)


def _mm_t_kernel(w_ref, x_ref, *o_refs, scale):
    acc = _dot_nt(w_ref[...], x_ref[0].astype(BF16))
    if scale is not None:
        acc = acc * scale
    for o_ref in o_refs:
        o_ref[0] = acc.astype(o_ref.dtype)


def _mm_t(w_t, x, out_dtypes, row_start=0, rows=None, scale=None, tm=512, tn=512):
    b, t, k = x.shape
    m = w_t.shape[0]
    rows = t - row_start if rows is None else rows
    tm, tn = min(tm, rows), min(tn, m)
    off = row_start // tm
    return pl.pallas_call(
        functools.partial(_mm_t_kernel, scale=scale),
        grid=(b, rows // tm, m // tn),
        in_specs=[pl.BlockSpec((tn, k), lambda bi, i, j: (j, 0)),
                  pl.BlockSpec((1, tm, k), lambda bi, i, j: (bi, i + off, 0))],
        out_specs=[pl.BlockSpec((1, tn, tm), lambda bi, i, j: (bi, j, i)) for _ in out_dtypes],
        out_shape=[jax.ShapeDtypeStruct((b, m, rows), dt) for dt in out_dtypes],
        compiler_params=_params("parallel", "parallel", "parallel"),
        name="mm_t",
    )(w_t, x)


def _log_sigmoid(y):
    return jnp.minimum(y, 0.0) - jnp.log1p(jnp.exp(-jnp.abs(y)))


def _gate_kernel(x_ref, wft_ref, wf_ref, bfc_ref, bfr_ref, tri_ref, lf_ref, cum_ref, carry_ref):
    @pl.when(pl.program_id(1) == 0)
    def _():
        carry_ref[...] = jnp.zeros_like(carry_ref)

    xb = x_ref[0].astype(BF16)
    lf_ref[0] = _log_sigmoid(_dot_nt(wft_ref[...], xb) + bfc_ref[...])
    lf = _log_sigmoid(_dot(xb, wf_ref[...]) + bfr_ref[...])
    hi, mid, lo = _split3(lf)
    tri = tri_ref[...]
    cs = _dot(tri, hi) + _dot(tri, mid) + _dot(tri, lo) + carry_ref[0:1, :]
    cum_ref[0] = cs * LOG2E
    carry_ref[...] = jnp.broadcast_to(cs[cs.shape[0] - 1:, :], carry_ref.shape)


def _gate(x, wf_t, wf, b_f, tm=512):
    b, t, k = x.shape
    h = wf_t.shape[0]
    tm = min(tm, t)
    r = jnp.arange(tm)
    tri = (r[None, :] <= r[:, None]).astype(BF16)
    fixed = lambda shape: pl.BlockSpec(shape, lambda bi, i: (0, 0))
    return pl.pallas_call(
        _gate_kernel,
        grid=(b, t // tm),
        in_specs=[pl.BlockSpec((1, tm, k), lambda bi, i: (bi, i, 0)),
                  fixed((h, k)), fixed((k, h)), fixed((h, 1)), fixed((1, h)), fixed((tm, tm))],
        out_specs=[pl.BlockSpec((1, h, tm), lambda bi, i: (bi, 0, i)),
                   pl.BlockSpec((1, tm, h), lambda bi, i: (bi, i, 0))],
        out_shape=[jax.ShapeDtypeStruct((b, h, t), F32), jax.ShapeDtypeStruct((b, t, h), F32)],
        scratch_shapes=[pltpu.VMEM((8, h), F32)],
        compiler_params=_params("arbitrary", "arbitrary"),
        name="gate",
    )(x, wf_t, wf, b_f.reshape(h, 1), b_f.reshape(1, h), tri)


def _fox_attn_kernel(qt_ref, k_ref, vt_ref, c_ref, o_ref,
                     s0_ref, s1_ref, p0_ref, p1_ref, al0_ref, al1_ref, acc_ref, m_ref, *, tq):
    iq = pl.program_id(2)
    qt = qt_ref[0]
    row = lax.broadcasted_iota(jnp.int32, qt.shape, 0)
    zero = jnp.zeros_like(qt)
    qtm = [jnp.where(row < HEAD_DIM, qt, zero), jnp.where(row >= HEAD_DIM, qt, zero)]
    ones = jnp.ones((BF16_ROWS, tq), BF16)
    heads = range(HEADS_PER_SLAB)

    s_refs, p_refs, al_refs = (s0_ref, s1_ref), (p0_ref, p1_ref), (al0_ref, al1_ref)

    def scores(j, slot):
        start = pl.multiple_of(j * tq, tq)
        kc = k_ref[0, pl.ds(start, tq), :]
        for a in heads:
            s_refs[slot][a] = _dot(kc, qtm[a]) - c_ref[0, 0, pl.ds(start, tq), a:a + 1]

    def softmax(slot, masked):
        for a in heads:
            st = s_refs[slot][a]
            if masked:
                krow = lax.broadcasted_iota(jnp.int32, st.shape, 0)
                qcol = lax.broadcasted_iota(jnp.int32, st.shape, 1)
                st = jnp.where(krow <= qcol, st, NEG)
            m_old = m_ref[a]
            m_new = jnp.maximum(m_old, jnp.max(st, axis=0, keepdims=True))
            m_ref[a] = m_new
            al_refs[slot][a] = jnp.exp2(m_old - m_new)
            p_refs[slot][a] = jnp.exp2(st - m_new[0:1, :]).astype(BF16)

    def pv(j, slot):
        start = pl.multiple_of(jnp.maximum(j, 0) * tq, tq)
        for a in heads:
            vta = jnp.concatenate([vt_ref[0, a * HEAD_DIM:(a + 1) * HEAD_DIM, pl.ds(start, tq)], ones], axis=0)
            acc_ref[a] = al_refs[slot][a, 0:1, :] * acc_ref[a] + _dot(vta, p_refs[slot][a])

    def stage(j, slot):
        scores(j + 1, 1 - slot)
        softmax(slot, False)
        pv(j - 1, 1 - slot)

    scores(0, 0)
    acc_ref[...] = jnp.zeros_like(acc_ref)
    m_ref[...] = jnp.full_like(m_ref, NEG)
    al1_ref[...] = jnp.ones_like(al1_ref)
    p1_ref[...] = jnp.zeros_like(p1_ref)

    def body(jj, _):
        stage(2 * jj, 0)
        stage(2 * jj + 1, 1)
        return 0

    lax.fori_loop(0, iq // 2, body, 0)
    odd = iq % 2 == 1

    @pl.when(odd)
    def _():
        stage(iq - 1, 0)

    for slot in (0, 1):
        @pl.when(odd == (slot == 1))
        def _():
            softmax(slot, True)
            pv(iq - 1, 1 - slot)
            pv(iq, slot)

    outs = [acc_ref[a, :HEAD_DIM, :] / acc_ref[a, HEAD_DIM:HEAD_DIM + 1, :] for a in heads]
    o_ref[0] = jnp.concatenate(outs, axis=0).astype(o_ref.dtype)


def _fox_attn(q_t, k, v_t, cum_tok, tq=512):
    b, d, t = q_t.shape
    n_slab = d // LANES
    tq = min(tq, t)
    c4 = cum_tok.reshape(b, t, n_slab, HEADS_PER_SLAB).transpose(0, 2, 1, 3)
    return pl.pallas_call(
        functools.partial(_fox_attn_kernel, tq=tq),
        grid=(b, n_slab, t // tq),
        in_specs=[pl.BlockSpec((1, LANES, tq), lambda bi, p, i: (bi, p, i)),
                  pl.BlockSpec((1, t, LANES), lambda bi, p, i: (bi, 0, p)),
                  pl.BlockSpec((1, LANES, t), lambda bi, p, i: (bi, p, 0)),
                  pl.BlockSpec((1, 1, t, HEADS_PER_SLAB), lambda bi, p, i: (bi, p, 0, 0))],
        out_specs=pl.BlockSpec((1, LANES, tq), lambda bi, p, i: (bi, p, i)),
        out_shape=jax.ShapeDtypeStruct((b, d, t), BF16),
        scratch_shapes=([pltpu.VMEM((HEADS_PER_SLAB, tq, tq), F32)] * 2
                        + [pltpu.VMEM((HEADS_PER_SLAB, tq, tq), BF16)] * 2
                        + [pltpu.VMEM((HEADS_PER_SLAB, 8, tq), F32)] * 2
                        + [pltpu.VMEM((HEADS_PER_SLAB, HEAD_DIM + BF16_ROWS, tq), F32),
                           pltpu.VMEM((HEADS_PER_SLAB, 8, tq), F32)]),
        compiler_params=_params("parallel", "parallel", "arbitrary"),
        name="fox_attn",
    )(q_t, k, v_t, c4)


def _proj_norm_t_kernel(at_ref, w_ref, x_ref, g_ref, b_ref, o_ref, *, alpha):
    mix = lax.dot_general(at_ref[0], w_ref[...], (((0,), (0,)), ((), ())), preferred_element_type=F32)
    o_ref[0] = _layer_norm(alpha * x_ref[0] + mix, g_ref[...], b_ref[...])


def _proj_norm_t(a_t, w, x, g, b, alpha, tm=512):
    nb, ka, t = a_t.shape
    d = w.shape[1]
    tm = min(tm, t)
    fixed = lambda shape: pl.BlockSpec(shape, lambda bi, i: (0, 0))
    return pl.pallas_call(
        functools.partial(_proj_norm_t_kernel, alpha=alpha),
        grid=(nb, t // tm),
        in_specs=[pl.BlockSpec((1, ka, tm), lambda bi, i: (bi, 0, i)),
                  fixed((ka, d)),
                  pl.BlockSpec((1, tm, d), lambda bi, i: (bi, i, 0)),
                  fixed((1, d)), fixed((1, d))],
        out_specs=pl.BlockSpec((1, tm, d), lambda bi, i: (bi, i, 0)),
        out_shape=jax.ShapeDtypeStruct((nb, t, d), F32),
        compiler_params=_params("parallel", "parallel"),
        name="proj_norm_t",
    )(a_t, w, x, g, b)


def _route(logits):
    lane = lax.broadcasted_iota(jnp.int32, logits.shape, 1).astype(F32)
    far = float(LANES)
    gl = jnp.where(lane < N_EXPERT_GROUPS, logits, NEG)
    gmax = jnp.max(gl, axis=1, keepdims=True)
    gsel = jnp.min(jnp.where(gl == gmax, lane, far), axis=1, keepdims=True)
    psel = 1.0 / jnp.sum(jnp.exp(gl - gmax), axis=1, keepdims=True)
    lo = N_EXPERT_GROUPS + gsel * EXPERTS_PER_GROUP
    el = jnp.where((lane >= lo) & (lane < lo + EXPERTS_PER_GROUP), logits, NEG)
    v1 = jnp.max(el, axis=1, keepdims=True)
    i1 = jnp.min(jnp.where(el == v1, lane, far), axis=1, keepdims=True)
    el2 = jnp.where(lane == i1, NEG, el)
    v2 = jnp.max(el2, axis=1, keepdims=True)
    i2 = jnp.min(jnp.where(el2 == v2, lane, far), axis=1, keepdims=True)
    e2 = jnp.exp(v2 - v1)
    den = 1.0 + e2
    return jnp.where(lane == i1, psel / den, jnp.where(lane == i2, psel * e2 / den, 0.0))


def _moe_kernel(x_ref, wr_ref, br_ref, wg_ref, wu_ref, wd_ref, g_ref, b_ref, o_ref,
                xb_ref, gates_ref, acc_ref, *, alpha):
    e = pl.program_id(1)

    @pl.when(e == 0)
    def _():
        x = x_ref[...]
        xh = x.astype(BF16)
        xl = (x - xh.astype(F32)).astype(BF16)
        wr = wr_ref[...]
        wh = wr.astype(BF16)
        wl = (wr - wh.astype(F32)).astype(BF16)
        xb_ref[...] = xh
        logits = _dot(xh, wh) + _dot(xl, wh) + _dot(xh, wl) + br_ref[...]
        gates_ref[...] = _route(logits)
        acc_ref[...] = jnp.zeros_like(acc_ref)

    xb = xb_ref[...]
    hg = _dot(xb, wg_ref[0].astype(BF16))
    hu = _dot(xb, wu_ref[0].astype(BF16))
    gcol = _lane_col(gates_ref[...], e + N_EXPERT_GROUPS)
    h = hg * (1.0 / (1.0 + jnp.exp(-hg))) * hu
    acc_ref[...] += _dot((h * gcol).astype(BF16), wd_ref[0].astype(BF16))

    @pl.when(e == pl.num_programs(1) - 1)
    def _():
        o_ref[...] = _layer_norm(alpha * x_ref[...] + acc_ref[...], g_ref[...], b_ref[...])


def _moe_norm(x, wr, br, wg, wu, wd, g, b, alpha, tm=1024):
    n, d = x.shape
    ne, _, f = wg.shape
    tm = min(tm, n)
    return pl.pallas_call(
        functools.partial(_moe_kernel, alpha=alpha),
        grid=(n // tm, ne),
        in_specs=[pl.BlockSpec((tm, d), lambda i, e: (i, 0)),
                  pl.BlockSpec((d, LANES), lambda i, e: (0, 0)),
                  pl.BlockSpec((1, LANES), lambda i, e: (0, 0)),
                  pl.BlockSpec((1, d, f), lambda i, e: (e, 0, 0)),
                  pl.BlockSpec((1, d, f), lambda i, e: (e, 0, 0)),
                  pl.BlockSpec((1, f, d), lambda i, e: (e, 0, 0)),
                  pl.BlockSpec((1, d), lambda i, e: (0, 0)),
                  pl.BlockSpec((1, d), lambda i, e: (0, 0))],
        out_specs=pl.BlockSpec((tm, d), lambda i, e: (i, 0)),
        out_shape=jax.ShapeDtypeStruct((n, d), F32),
        scratch_shapes=[pltpu.VMEM((tm, d), BF16), pltpu.VMEM((tm, LANES), F32), pltpu.VMEM((tm, d), F32)],
        compiler_params=_params("parallel", "arbitrary"),
        name="moe_norm",
    )(x, wr, br, wg, wu, wd, g, b)


def _route_t_kernel(x_ref, wr_ref, br_ref, u_ref, route_ref, cnt_ref, carry_ref):
    @pl.when(pl.program_id(0) == 0)
    def _():
        carry_ref[...] = jnp.zeros_like(carry_ref)

    x = x_ref[...]
    xh = x.astype(BF16)
    xl = (x - xh.astype(F32)).astype(BF16)
    w = wr_ref[...]
    wh = w.astype(BF16)
    wl = (w - wh.astype(F32)).astype(BF16)
    logits = _dot_nt(wh, xh) + _dot_nt(wh, xl) + _dot_nt(wl, xh) + br_ref[...]
    r = lax.broadcasted_iota(jnp.int32, logits.shape, 0).astype(F32)
    far = float(logits.shape[0])
    gl = jnp.where(r < N_EXPERT_GROUPS, logits, NEG)
    gmax = jnp.max(gl, axis=0, keepdims=True)
    gsel = jnp.min(jnp.where(gl == gmax, r, far), axis=0, keepdims=True)
    psel = 1.0 / jnp.sum(jnp.exp(gl - gmax), axis=0, keepdims=True)
    lo = N_EXPERT_GROUPS + gsel * EXPERTS_PER_GROUP
    el = jnp.where((r >= lo) & (r < lo + EXPERTS_PER_GROUP), logits, NEG)
    v1 = jnp.max(el, axis=0, keepdims=True)
    i1 = jnp.min(jnp.where(el == v1, r, far), axis=0, keepdims=True)
    el2 = jnp.where(r == i1, NEG, el)
    v2 = jnp.max(el2, axis=0, keepdims=True)
    i2 = jnp.min(jnp.where(el2 == v2, r, far), axis=0, keepdims=True)
    e2 = jnp.exp(v2 - v1)
    den = 1.0 + e2
    onehot = jnp.where(r == i1, 1.0, jnp.where(r == i2, 1.0, 0.0)).astype(BF16)
    cum = _dot(onehot, u_ref[...]) + carry_ref[:, 0:1]
    rank1 = jnp.sum(jnp.where(r == i1, cum, 0.0), axis=0, keepdims=True) - 1.0
    rank2 = jnp.sum(jnp.where(r == i2, cum, 0.0), axis=0, keepdims=True) - 1.0
    carry_ref[...] = jnp.broadcast_to(cum[:, cum.shape[1] - 1:], carry_ref.shape)
    cnt_ref[...] = carry_ref[...]
    rows = (i1 - N_EXPERT_GROUPS, i2 - N_EXPERT_GROUPS, psel / den, psel * e2 / den, rank1, rank2)
    r8 = lax.broadcasted_iota(jnp.int32, route_ref.shape, 0)
    out = jnp.zeros(route_ref.shape, F32)
    for k, v in enumerate(rows):
        out = jnp.where(r8 == k, v, out)
    route_ref[...] = out


def _route_t(x, wr_t, br_t, tm=512):
    n, d = x.shape
    nr = wr_t.shape[0]
    tm = min(tm, n)
    r = jnp.arange(tm)
    upper = (r[:, None] <= r[None, :]).astype(BF16)
    fixed = lambda shape: pl.BlockSpec(shape, lambda i: (0, 0))
    return pl.pallas_call(
        _route_t_kernel,
        grid=(n // tm,),
        in_specs=[pl.BlockSpec((tm, d), lambda i: (i, 0)), fixed((nr, d)), fixed((nr, 1)), fixed((tm, tm))],
        out_specs=[pl.BlockSpec((8, tm), lambda i: (0, i)), fixed((nr, LANES))],
        out_shape=[jax.ShapeDtypeStruct((8, n), F32), jax.ShapeDtypeStruct((nr, LANES), F32)],
        scratch_shapes=[pltpu.VMEM((nr, LANES), F32)],
        compiler_params=_params("arbitrary"),
        name="route",
    )(x, wr_t, br_t, upper)


def _row_copies(idx_ref, tt, src_of, dst_of, sem):
    def each(fn):
        def body(j, carry):
            for k in range(2):
                p = idx_ref[k * tt + j]
                fn(pltpu.make_async_copy(src_of(k, j, p), dst_of(k, j, p), sem))
            return carry
        lax.fori_loop(0, tt, body, 0, unroll=8)
    each(lambda cp: cp.start())
    return lambda: each(lambda cp: cp.wait())


def _dispatch_kernel(pos_ref, x_ref, xs_in_ref, xs_ref, idx_ref, sem_i, sem_r, *, tt):
    del xs_in_ref
    i = pl.program_id(0)
    cp = pltpu.make_async_copy(pos_ref.at[pl.ds(i * 2 * tt, 2 * tt)], idx_ref, sem_i)
    cp.start()
    cp.wait()
    wait = _row_copies(idx_ref, tt,
                       lambda k, j, p: x_ref.at[pl.ds(i * tt + j, 1), :],
                       lambda k, j, p: xs_ref.at[pl.ds(p, 1), :], sem_r)
    wait()


def _dispatch(pos_flat, x, n_rows, tt):
    n, d = x.shape
    return pl.pallas_call(
        functools.partial(_dispatch_kernel, tt=tt),
        grid=(n // tt,),
        in_specs=[pl.BlockSpec(memory_space=pl.ANY)] * 3,
        out_specs=pl.BlockSpec(memory_space=pl.ANY),
        out_shape=jax.ShapeDtypeStruct((n_rows, d), F32),
        scratch_shapes=[pltpu.SMEM((2 * tt,), jnp.int32), pltpu.SemaphoreType.DMA(()), pltpu.SemaphoreType.DMA(())],
        input_output_aliases={2: 0},
        compiler_params=_params("arbitrary"),
        name="dispatch",
    )(pos_flat, x, jnp.zeros((n_rows, d), F32))


def _expert_kernel(te_ref, blk_ref, nv_ref, xs_ref, wg_ref, wu_ref, wd_ref, ys_ref, wgb, wub, wdb):
    i = pl.program_id(0)

    @pl.when((i == 0) | (te_ref[i] != te_ref[jnp.maximum(i - 1, 0)]))
    def _():
        wgb[...] = wg_ref[0].astype(BF16)
        wub[...] = wu_ref[0].astype(BF16)
        wdb[...] = wd_ref[0].astype(BF16)

    @pl.when(i < nv_ref[0])
    def _():
        xb = xs_ref[...].astype(BF16)
        hg = _dot(xb, wgb[...])
        hu = _dot(xb, wub[...])
        h = hg * (1.0 / (1.0 + jnp.exp(-hg))) * hu
        ys_ref[...] = _dot(h.astype(BF16), wdb[...])

    @pl.when(i >= nv_ref[0])
    def _():
        ys_ref[...] = jnp.zeros_like(ys_ref)


def _experts(tile_expert, tile_block, n_valid, xs, wg, wu, wd, rows):
    n_rows, d = xs.shape
    f = wg.shape[2]
    n_tiles = tile_expert.shape[0]
    row_spec = pl.BlockSpec((rows, d), lambda i, te, blk, nv: (blk[i], 0))
    grid_spec = pltpu.PrefetchScalarGridSpec(
        num_scalar_prefetch=3,
        grid=(n_tiles,),
        in_specs=[row_spec,
                  pl.BlockSpec((1, d, f), lambda i, te, blk, nv: (te[i], 0, 0)),
                  pl.BlockSpec((1, d, f), lambda i, te, blk, nv: (te[i], 0, 0)),
                  pl.BlockSpec((1, f, d), lambda i, te, blk, nv: (te[i], 0, 0))],
        out_specs=row_spec,
        scratch_shapes=[pltpu.VMEM((d, f), BF16), pltpu.VMEM((d, f), BF16), pltpu.VMEM((f, d), BF16)],
    )
    return pl.pallas_call(
        _expert_kernel,
        grid_spec=grid_spec,
        out_shape=jax.ShapeDtypeStruct((n_rows, d), F32),
        compiler_params=_params("arbitrary"),
        name="experts",
    )(tile_expert, tile_block, n_valid, xs, wg, wu, wd)


def _combine_kernel(pos_ref, ys_ref, x_ref, gs_ref, sel_ref, g_ref, b_ref, o_ref, idx_ref, ybuf, sem_i, sem_r,
                    *, tt, alpha):
    i = pl.program_id(0)
    cp = pltpu.make_async_copy(pos_ref.at[pl.ds(i * 2 * tt, 2 * tt)], idx_ref, sem_i)
    cp.start()
    cp.wait()
    wait = _row_copies(idx_ref, tt,
                       lambda k, j, p: ys_ref.at[pl.ds(p, 1), :],
                       lambda k, j, p: ybuf.at[k, pl.ds(j, 1), :], sem_r)
    wcols = lax.dot_general(gs_ref[...], sel_ref[...], (((0,), (0,)), ((), ())), preferred_element_type=F32)
    w1, w2 = wcols[:, 0:1], wcols[:, LANES:LANES + 1]
    wait()
    z = alpha * x_ref[...] + w1 * ybuf[0] + w2 * ybuf[1]
    o_ref[...] = _layer_norm(z, g_ref[...], b_ref[...])


def _combine(pos_flat, ys, x, gate_split, g, b, alpha, tt):
    n, d = x.shape
    sel = jnp.zeros((8, 2 * LANES), BF16).at[0:2, :LANES].set(1).at[2:4, LANES:].set(1)
    fixed = lambda shape: pl.BlockSpec(shape, lambda i: (0, 0))
    return pl.pallas_call(
        functools.partial(_combine_kernel, tt=tt, alpha=alpha),
        grid=(n // tt,),
        in_specs=[pl.BlockSpec(memory_space=pl.ANY), pl.BlockSpec(memory_space=pl.ANY),
                  pl.BlockSpec((tt, d), lambda i: (i, 0)),
                  pl.BlockSpec((8, tt), lambda i: (0, i)),
                  fixed((8, 2 * LANES)), fixed((1, d)), fixed((1, d))],
        out_specs=pl.BlockSpec((tt, d), lambda i: (i, 0)),
        out_shape=jax.ShapeDtypeStruct((n, d), F32),
        scratch_shapes=[pltpu.SMEM((2 * tt,), jnp.int32), pltpu.VMEM((2, tt, d), F32),
                        pltpu.SemaphoreType.DMA(()), pltpu.SemaphoreType.DMA(())],
        compiler_params=_params("arbitrary"),
        name="combine",
    )(pos_flat, ys, x, gate_split, sel, g, b)


def _moe_routed_norm(x, wr_t, br_t, wg, wu, wd, g, b, alpha, rows=256, tt=512):
    n, d = x.shape
    ne = wg.shape[0]
    route, counts = _route_t(x, wr_t, br_t)
    expert = route[0:2].astype(jnp.int32)
    rank = route[4:6].astype(jnp.int32)
    cnt = counts[N_EXPERT_GROUPS:N_EXPERT_GROUPS + ne, 0].astype(jnp.int32)
    tiles = (cnt + rows - 1) // rows
    tile_end = jnp.cumsum(tiles)
    pos = ((tile_end - tiles) * rows)[expert] + rank
    pos_flat = pos.reshape(2, n // tt, tt).transpose(1, 0, 2).reshape(-1)
    max_tiles = 2 * n // rows + ne
    n_valid = tile_end[ne - 1]
    tile_block = jnp.arange(max_tiles, dtype=jnp.int32)
    last = jnp.maximum(n_valid - 1, 0)
    tile_expert = jnp.minimum(jnp.searchsorted(tile_end, jnp.minimum(tile_block, last), side="right"),
                              ne - 1).astype(jnp.int32)
    gate = route[2:4]
    g_hi = gate.astype(BF16)
    g_lo = (gate - g_hi.astype(F32)).astype(BF16)
    gate_split = jnp.concatenate([g_hi[0:1], g_lo[0:1], g_hi[1:2], g_lo[1:2], jnp.zeros((4, n), BF16)], axis=0)

    xs = _dispatch(pos_flat, x, max_tiles * rows, tt)
    ys = _experts(tile_expert, tile_block, n_valid.reshape(1).astype(jnp.int32), xs, wg, wu, wd, rows)
    return _combine(pos_flat, ys, x, gate_split, g, b, alpha, tt)


def _alibi_slope(g, h, n_heads):
    return 2.0 ** (-8.0 * (h * N_DIL_GROUPS + g + 1) / (N_DIL_GROUPS * n_heads))


def _mm_dil_kernel(x_ref, w_ref, o_ref, acc_ref, *, rate):
    acc = _dot(x_ref[0].astype(BF16), w_ref[...])
    if rate == 1:
        o_ref[0, 0] = acc.astype(o_ref.dtype)
        return
    rows = acc.shape[0] // rate
    for c in range(acc_ref.shape[0]):
        acc_ref[c] = acc[:, c * LANES:(c + 1) * LANES]
    for r in range(rate):
        for c in range(acc_ref.shape[0]):
            o_ref[0, r, :, c * LANES:(c + 1) * LANES] = \
                acc_ref[c, pl.ds(r, rows, stride=rate), :].astype(o_ref.dtype)


def _mm_dil(x, w, rate, tm=1024, tn=512):
    b, t, k = x.shape
    m = w.shape[1]
    tm, tn = min(tm, t), min(tn, m)
    return pl.pallas_call(
        functools.partial(_mm_dil_kernel, rate=rate),
        grid=(b, t // tm, m // tn),
        in_specs=[pl.BlockSpec((1, tm, k), lambda bi, i, j: (bi, i, 0)),
                  pl.BlockSpec((k, tn), lambda bi, i, j: (0, j))],
        out_specs=pl.BlockSpec((1, rate, tm // rate, tn), lambda bi, i, j: (bi, 0, i, j)),
        out_shape=jax.ShapeDtypeStruct((b, rate, t // rate, m), BF16),
        scratch_shapes=[pltpu.VMEM((tn // LANES, tm, LANES), F32)],
        compiler_params=_params("parallel", "parallel", "parallel"),
        name=f"mm_dil{rate}",
    )(x, w)


def _dil_attn_kernel(q_ref, kc_ref, kp_ref, vc_ref, vp_ref, o_ref, lse_ref, kbuf, vbuf, *, tu, rate, slopes):
    i = pl.program_id(2)
    kbuf[0:DIL_KEYS] = kp_ref[0, 0]
    kbuf[DIL_KEYS:] = kc_ref[0, 0]
    vbuf[0:DIL_KEYS] = vp_ref[0, 0]
    vbuf[DIL_KEYS:] = vc_ref[0, 0]
    shape = (DIL_KEYS, 2 * DIL_KEYS)
    a = lax.broadcasted_iota(jnp.int32, shape, 0)
    c = lax.broadcasted_iota(jnp.int32, shape, 1)
    dist = a - c + DIL_KEYS
    base = jnp.where((dist >= 0) & (dist <= DIL_KEYS), (-rate * dist).astype(F32), NEG)
    base_first = jnp.where(c >= DIL_KEYS, base, NEG)
    lane = lax.broadcasted_iota(jnp.int32, (DIL_KEYS, LANES), 1)
    left = lane < HEAD_DIM
    for s in range(tu // DIL_KEYS):
        bias = jnp.where(i == 0, base_first, base) if s == 0 else base
        r0 = s * DIL_KEYS
        for p in range(q_ref.shape[3] // LANES):
            cs = slice(p * LANES, (p + 1) * LANES)
            qp = q_ref[0, 0, r0:r0 + DIL_KEYS, cs]
            kk = kbuf[r0:r0 + 2 * DIL_KEYS, cs]
            vv = vbuf[r0:r0 + 2 * DIL_KEYS, cs]
            outs, lses = [], []
            for hh in range(HEADS_PER_SLAB):
                qm = jnp.where(left if hh == 0 else jnp.logical_not(left), qp, jnp.zeros_like(qp))
                sc = _dot_nt(qm, kk) + slopes[HEADS_PER_SLAB * p + hh] * bias
                m = jnp.max(sc, axis=1, keepdims=True)
                e = jnp.exp(sc - m)
                l = jnp.sum(e, axis=1, keepdims=True)
                outs.append(_dot(e.astype(BF16), vv) / l)
                lses.append(m + jnp.log(l))
            o_ref[0, 0, r0:r0 + DIL_KEYS, cs] = jnp.where(left, outs[0], outs[1])
            lse_ref[0, 0, r0:r0 + DIL_KEYS, cs] = jnp.where(left, lses[0], lses[1])


def _dil_attn(qkv, g, tu=512):
    b, rate, u, dg3 = qkv.shape
    dg = dg3 // 3
    tu = min(tu, u)
    ratio = tu // DIL_KEYS
    slopes = tuple(_alibi_slope(g, h, dg // HEAD_DIM) for h in range(dg // HEAD_DIM))
    cur = lambda col: pl.BlockSpec((1, 1, tu, dg), lambda bi, r, i: (bi, r, i, col))
    prev = lambda col: pl.BlockSpec((1, 1, DIL_KEYS, dg),
                                    lambda bi, r, i: (bi, r, jnp.maximum(i * ratio - 1, 0), col))
    return pl.pallas_call(
        functools.partial(_dil_attn_kernel, tu=tu, rate=rate, slopes=slopes),
        grid=(b, rate, u // tu),
        in_specs=[cur(0), cur(1), prev(1), cur(2), prev(2)],
        out_specs=[cur(0)] * 2,
        out_shape=[jax.ShapeDtypeStruct((b, rate, u, dg), F32)] * 2,
        scratch_shapes=[pltpu.VMEM((tu + DIL_KEYS, dg), BF16)] * 2,
        compiler_params=_params("parallel", "parallel", "arbitrary"),
        name=f"dil_attn{g}",
    )(qkv, qkv, qkv, qkv, qkv)


def _mix_norm_kernel(*refs, alpha, rates):
    ng = len(rates)
    o_refs, l_refs = refs[:ng], refs[ng:2 * ng]
    w_ref, x_ref, g_ref, b_ref, out_ref = refs[2 * ng:2 * ng + 5]
    bufs = refs[2 * ng + 5:]

    def natural(ref, buf, rate):
        if rate == 1:
            return ref[0, 0]
        for r in range(rate):
            for c in range(buf.shape[0]):
                buf[c, pl.ds(r, ref.shape[2], stride=rate), :] = ref[0, r, :, c * LANES:(c + 1) * LANES]
        return jnp.concatenate([buf[c] for c in range(buf.shape[0])], axis=1)

    os_ = [natural(o_refs[g], bufs[g], rates[g]) for g in range(ng)]
    ls = [natural(l_refs[g], bufs[ng + g], rates[g]) for g in range(ng)]
    m = functools.reduce(jnp.maximum, ls)
    es = [jnp.exp(l - m) for l in ls]
    o = sum(e * o_ for e, o_ in zip(es, os_)) / sum(es)
    mix = _dot(o.astype(BF16), w_ref[...])
    out_ref[0] = _layer_norm(alpha * x_ref[0] + mix, g_ref[...], b_ref[...])


def _mix_norm(os_, ls, w, x, g, b, alpha, tm=512):
    nb, t, d = x.shape
    dg = w.shape[0]
    tm = min(tm, t)
    rates = tuple(o.shape[1] for o in os_)
    grp = lambda rate: pl.BlockSpec((1, rate, tm // rate, dg), lambda bi, i: (bi, 0, i, 0))
    fixed = lambda shape: pl.BlockSpec(shape, lambda bi, i: (0, 0))
    xrow = pl.BlockSpec((1, tm, d), lambda bi, i: (bi, i, 0))
    return pl.pallas_call(
        functools.partial(_mix_norm_kernel, alpha=alpha, rates=rates),
        grid=(nb, t // tm),
        in_specs=[grp(r) for r in rates] * 2 + [fixed((dg, d)), xrow, fixed((1, d)), fixed((1, d))],
        out_specs=xrow,
        out_shape=jax.ShapeDtypeStruct((nb, t, d), F32),
        scratch_shapes=[pltpu.VMEM((dg // LANES, tm, LANES), F32)] * (2 * len(rates)),
        compiler_params=_params("parallel", "parallel"),
        name="mix_norm",
    )(*os_, *ls, w, x, g, b)


def _fox_decode_kernel(pt_ref, qt_ref, u_ref, *refs, pages):
    k_refs, v_refs, lf_refs = refs[:pages], refs[pages:2 * pages], refs[2 * pages:3 * pages]
    ot_ref, mt_ref, lt_ref, ct_ref, qcol_ref, acc_ref, m_ref, l_ref, c_ref = refs[3 * pages:]
    b, j = pl.program_id(0), pl.program_id(1)
    nh = m_ref.shape[0]

    @pl.when((b == 0) & (j == 0))
    def _():
        for r in (ot_ref, mt_ref, lt_ref, ct_ref):
            r[...] = jnp.zeros_like(r)

    @pl.when(j == 0)
    def _():
        qcol = _lane_col(qt_ref[...], b) * ATTN_SCALE
        qcol_ref[...] = jnp.broadcast_to(qcol.astype(BF16).astype(F32), qcol_ref.shape)
        acc_ref[...] = jnp.zeros_like(acc_ref)
        m_ref[...] = jnp.full_like(m_ref, NEG)
        l_ref[...] = jnp.zeros_like(l_ref)
        c_ref[...] = jnp.zeros_like(c_ref)

    u = u_ref[...]
    for pi in range(pages):
        hi, mid, lo = _split3(lf_refs[pi][0])
        cs = _dot(hi, u) + _dot(mid, u) + _dot(lo, u) + c_ref[...]
        c_ref[...] = jnp.broadcast_to(cs[:, LANES - 1:LANES], c_ref.shape)
        s = _seg_sum(qcol_ref[...] * k_refs[pi][0], HEAD_DIM) - cs
        m_old = m_ref[...]
        m_new = jnp.maximum(m_old, jnp.max(s, axis=1, keepdims=True))
        alpha = jnp.exp(m_old - m_new)
        p = jnp.exp(s - m_new)
        m_ref[...] = m_new
        l_ref[...] = alpha * l_ref[...] + p
        acc_ref[...] = (acc_ref[...] * _seg_expand(alpha, HEAD_DIM)
                        + _seg_expand(p, HEAD_DIM) * v_refs[pi][0])

    @pl.when(j == pl.num_programs(1) - 1)
    def _():
        ot_ref[...] = _put_lane(ot_ref[...], jnp.sum(acc_ref[...], axis=1, keepdims=True), b)
        lt_ref[...] = _put_lane(lt_ref[...], jnp.sum(l_ref[...], axis=1, keepdims=True), b)
        mt_ref[...] = _put_lane(mt_ref[...], m_ref[:, 0:1], b)
        ct_ref[...] = _put_lane(ct_ref[...], c_ref[:, 0:1], b)


def _fox_decode(page_table, q_t, k_pages, v_pages, lf_pages, pages=8):
    d, nb = q_t.shape
    nh = lf_pages.shape[1]
    n_pages = page_table.shape[1]
    ps = k_pages.shape[2]
    pt = page_table.reshape(-1)

    def page_spec(rows, pi):
        return pl.BlockSpec((1, rows, ps), lambda b, j, pt_ref: (pt_ref[b * n_pages + j * pages + pi], 0, 0))

    fixed = lambda shape: pl.BlockSpec(shape, lambda b, j, pt_ref: (0, 0))
    grid_spec = pltpu.PrefetchScalarGridSpec(
        num_scalar_prefetch=1,
        grid=(nb, n_pages // pages),
        in_specs=([fixed((d, nb)), fixed((LANES, LANES))]
                  + [page_spec(d, pi) for pi in range(pages)] * 2
                  + [page_spec(nh, pi) for pi in range(pages)]),
        out_specs=[fixed((d, nb)), fixed((nh, nb)), fixed((nh, nb)), fixed((nh, nb))],
        scratch_shapes=[pltpu.VMEM((d, ps), F32), pltpu.VMEM((d, ps), F32),
                        pltpu.VMEM((nh, ps), F32), pltpu.VMEM((nh, ps), F32), pltpu.VMEM((nh, ps), F32)],
    )
    return pl.pallas_call(
        functools.partial(_fox_decode_kernel, pages=pages),
        grid_spec=grid_spec,
        out_shape=[jax.ShapeDtypeStruct((d, nb), F32)] + [jax.ShapeDtypeStruct((nh, nb), F32)] * 3,
        compiler_params=_params("arbitrary", "arbitrary"),
        name="fox_decode",
    )(pt, q_t, _upper_tri(), *([k_pages] * pages), *([v_pages] * pages), *([lf_pages] * pages))


def _fox_merge_kernel(ot_ref, mt_ref, lt_ref, ct_ref, qt_ref, kn_ref, vn_ref, lfn_ref,
                      w_ref, x_ref, g_ref, b_ref, out_ref, *, alpha):
    qs = (qt_ref[...] * ATTN_SCALE).astype(BF16).astype(F32)
    s_new = _seg_sum(qs * kn_ref[...], HEAD_DIM) - (ct_ref[...] + lfn_ref[...])
    m_p = mt_ref[...]
    m = jnp.maximum(m_p, s_new)
    a_p = jnp.exp(m_p - m)
    a_n = jnp.exp(s_new - m)
    l = lt_ref[...] * a_p + a_n
    o_t = (ot_ref[...] * _seg_expand(a_p, HEAD_DIM) + _seg_expand(a_n, HEAD_DIM) * vn_ref[...]) \
        / _seg_expand(l, HEAD_DIM)
    mix = _dot(o_t.T.astype(BF16), w_ref[...])
    out_ref[...] = _layer_norm(alpha * x_ref[...] + mix, g_ref[...], b_ref[...])


def _fox_merge(o_t, m_t, l_t, c_t, q_t, kn_t, vn_t, lfn_t, w, x, g, b, alpha):
    return pl.pallas_call(
        functools.partial(_fox_merge_kernel, alpha=alpha),
        out_shape=jax.ShapeDtypeStruct(x.shape, F32),
        compiler_params=pltpu.CompilerParams(vmem_limit_bytes=VMEM_LIMIT),
        name="fox_merge",
    )(o_t, m_t, l_t, c_t, q_t, kn_t, vn_t, lfn_t, w, x, g, b)


def _dil_decode_kernel(buf_ref, qt_ref, kvn_ref, sl_ref, out_ref, ot_ref, mt_ref, lt_ref, p_ref,
                       *, window, rate):
    bi, c = pl.program_id(0), pl.program_id(1)

    @pl.when((bi == 0) & (c == 0))
    def _():
        for r in (ot_ref, mt_ref, lt_ref):
            r[...] = jnp.zeros_like(r)

    t = lax.broadcasted_iota(jnp.int32, (1, window), 1)
    for bb in range(buf_ref.shape[0]):
        b = bi * buf_ref.shape[0] + bb
        x = buf_ref[bb]

        @pl.when(c == 0)
        def _():
            qcol = (_lane_col(qt_ref[...], b) * ATTN_SCALE).astype(BF16).astype(F32)
            s = _seg_sum(x * qcol, HEAD_DIM)
            dist = (window - t).astype(F32)
            s = jnp.where((t & (rate - 1)) == 0, s - sl_ref[:, 0:1] * dist, NEG)
            m = jnp.max(s, axis=1, keepdims=True)
            p = jnp.exp(s - m)
            p_ref[bb] = p
            mt_ref[...] = _put_lane(mt_ref[...], m, b)
            lt_ref[...] = _put_lane(lt_ref[...], jnp.sum(p, axis=1, keepdims=True), b)

        @pl.when(c == 1)
        def _():
            o = jnp.sum(x * _seg_expand(p_ref[bb], HEAD_DIM), axis=1, keepdims=True)
            ot_ref[...] = _put_lane(ot_ref[...], o, b)

        newcol = _lane_col(kvn_ref[...], b)
        shifted = pltpu.roll(x, window - 1, axis=1)
        out_ref[bb] = jnp.where(t == window - 1, newcol, shifted)


def _dil_decode(buf_t, qg_t, kvn_t, g):
    nb, _, window = buf_t.shape
    dg = qg_t.shape[0]
    nh = dg // HEAD_DIM
    rate = DIL_RATES[g]
    assert rate & (rate - 1) == 0 and window == DIL_KEYS * rate
    slopes = jnp.asarray([_alibi_slope(g, h, nh) for h in range(nh)], F32)
    sl = jnp.broadcast_to(slopes.reshape(nh, 1), (nh, LANES))
    fixed = lambda shape: pl.BlockSpec(shape, lambda b, c: (0, 0))
    per_step = max(1, min(nb, DECODE_BLOCK_BYTES // (dg * window * 4)))
    assert nb % per_step == 0
    return pl.pallas_call(
        functools.partial(_dil_decode_kernel, window=window, rate=rate),
        grid=(nb // per_step, 2),
        in_specs=[pl.BlockSpec((per_step, dg, window), lambda b, c: (b, c, 0)),
                  fixed((dg, nb)),
                  pl.BlockSpec((dg, nb), lambda b, c: (c, 0)),
                  fixed((nh, LANES))],
        out_specs=[pl.BlockSpec((per_step, dg, window), lambda b, c: (b, c, 0)),
                   fixed((dg, nb)), fixed((nh, nb)), fixed((nh, nb))],
        out_shape=[jax.ShapeDtypeStruct(buf_t.shape, F32),
                   jax.ShapeDtypeStruct((dg, nb), F32),
                   jax.ShapeDtypeStruct((nh, nb), F32),
                   jax.ShapeDtypeStruct((nh, nb), F32)],
        scratch_shapes=[pltpu.VMEM((per_step, nh, window), F32)],
        compiler_params=_params("arbitrary", "arbitrary"),
        name=f"dil_decode{g}",
    )(buf_t, qg_t, kvn_t, sl)


def _dil_merge_kernel(*refs, alpha):
    states = refs[:3 * N_DIL_GROUPS]
    qt_ref, kvn_ref, w_ref, x_ref, g_ref, b_ref, out_ref = refs[3 * N_DIL_GROUPS:]
    dg = w_ref.shape[0]
    outs, lses = [], []
    for g in range(N_DIL_GROUPS):
        ot_ref, mt_ref, lt_ref = states[3 * g:3 * g + 3]
        qs =(qt_ref[g * dg:(g + 1) * dg, :] * ATTN_SCALE).astype(BF16).astype(F32)
        kn = kvn_ref[2 * g * dg:(2 * g + 1) * dg, :]
        vn = kvn_ref[(2 * g + 1) * dg:(2 * g + 2) * dg, :]
        s_new = _seg_sum(qs * kn, HEAD_DIM)
        m_p = mt_ref[...]
        m = jnp.maximum(m_p, s_new)
        a_p = jnp.exp(m_p - m)
        a_n = jnp.exp(s_new - m)
        l = lt_ref[...] * a_p + a_n
        o = (ot_ref[...] * _seg_expand(a_p, HEAD_DIM) + _seg_expand(a_n, HEAD_DIM) * vn) \
            / _seg_expand(l, HEAD_DIM)
        outs.append(o)
        lses.append(m + jnp.log(l))
    m = jnp.maximum(jnp.maximum(lses[0], lses[1]), lses[2])
    es = [jnp.exp(l - m) for l in lses]
    tot = es[0] + es[1] + es[2]
    o_t = sum(_seg_expand(e / tot, HEAD_DIM) * o for e, o in zip(es, outs))
    mix = _dot(o_t.T.astype(BF16), w_ref[...])
    out_ref[...] = _layer_norm(alpha * x_ref[...] + mix, g_ref[...], b_ref[...])


def _dil_merge(states, q_t, kvn_t, w, x, g, b, alpha):
    flat = [a for st in states for a in st]
    return pl.pallas_call(
        functools.partial(_dil_merge_kernel, alpha=alpha),
        out_shape=jax.ShapeDtypeStruct(x.shape, F32),
        compiler_params=pltpu.CompilerParams(vmem_limit_bytes=VMEM_LIMIT),
        name="dil_merge",
    )(*flat, q_t, kvn_t, w, x, g, b)


def kernel(x_prompt, x_sample, cache_fox_k, cache_fox_v, cache_fox_logf, cache_dil_kv_0, cache_dil_kv_1,
           cache_dil_kv_2, page_table, w_qkvf_a, b_f_a, w_o_a, w_q_b, w_kv_b, w_o_b, ln_g, ln_b,
           w_router_grp, b_router_grp, w_router_exp, b_router_exp, w_gate, w_up, w_down):
    nb, t, d = x_prompt.shape
    ns = x_sample.shape[0]
    depth = w_gate.shape[0]
    alpha = (2 * depth) ** 0.25
    nh = d // HEAD_DIM
    dg = w_o_b.shape[1]
    dq = N_DIL_GROUPS * dg

    w_a = w_qkvf_a[0]
    wk = w_a[:, d:2 * d].astype(BF16)
    wqkv_t = w_a[:, :3 * d].T.astype(BF16)
    wf = w_a[:, 3 * d:].astype(BF16)
    wf_t = wf.T
    b_f = b_f_a[0]
    wo_a = w_o_a[0].astype(BF16)
    wqkv_b_t = jnp.concatenate([w_q_b[0], w_kv_b], axis=1).T.astype(BF16)
    wkv_t = w_kv_b.T.astype(BF16)
    wo_b = w_o_b[0].astype(BF16)
    wqkv_g = [jnp.concatenate([w_q_b[0][:, g * dg:(g + 1) * dg] * ATTN_SCALE,
                               w_kv_b[:, 2 * g * dg:(2 * g + 2) * dg]], axis=1).astype(BF16)
              for g in range(N_DIL_GROUPS)]
    pad = LANES - N_EXPERT_GROUPS - N_EXPERTS
    wr = [jnp.pad(jnp.concatenate([w_router_grp[l], w_router_exp[l]], axis=1), ((0, 0), (0, pad)))
          for l in range(depth)]
    br = [jnp.pad(jnp.concatenate([b_router_grp[l], b_router_exp[l]]), (0, pad)).reshape(1, LANES)
          for l in range(depth)]
    wr_t = [jnp.concatenate([w_router_grp[l].T, w_router_exp[l].T, jnp.zeros((4, d), F32)], axis=0)
            for l in range(depth)]
    br_t = [jnp.concatenate([b_router_grp[l], b_router_exp[l], jnp.zeros((4,), F32)]).reshape(-1, 1)
            for l in range(depth)]
    lng = lambda l, i: ln_g[l, i].reshape(1, d)
    lnb = lambda l, i: ln_b[l, i].reshape(1, d)

    def moe(x, l):
        return _moe_norm(x, wr[l], br[l], w_gate[l], w_up[l], w_down[l], lng(l, 1), lnb(l, 1), alpha)

    def moe_routed(x, l):
        return _moe_routed_norm(x, wr_t[l], br_t[l], w_gate[l], w_up[l], w_down[l], lng(l, 1), lnb(l, 1), alpha)

    xp = x_prompt.reshape(nb * t, d)
    (q_t,) = _mm_t(wqkv_t[:d], x_prompt, (BF16,), scale=ATTN_SCALE * LOG2E)
    k_rm = _mm(xp, wk, BF16).reshape(nb, t, d)
    (k_t,) = _mm_t(wqkv_t[d:2 * d], x_prompt, (F32,))
    v_t, v_tb = _mm_t(wqkv_t[2 * d:], x_prompt, (F32, BF16))
    lf_t, cum_tok = _gate(x_prompt, wf_t, wf, b_f)
    o_t = _fox_attn(q_t, k_rm, v_tb, cum_tok)
    x1 = _proj_norm_t(o_t, wo_a, x_prompt, lng(0, 0), lnb(0, 0), alpha)
    x2 = moe_routed(x1.reshape(nb * t, d), 0).reshape(nb, t, d)

    tail = min(max(DIL_WINDOWS), t)
    (kv_tail_t,) = _mm_t(wkv_t, x2, (F32,), row_start=t - tail, rows=tail)
    os_, ls_ = zip(*[_dil_attn(_mm_dil(x2, wqkv_g[g], DIL_RATES[g]), g) for g in range(N_DIL_GROUPS)])
    x3 = _mix_norm(os_, ls_, wo_b, x2, lng(1, 0), lnb(1, 0), alpha)
    y_prompt = moe_routed(x3.reshape(nb * t, d), 1).reshape(nb, t, d)

    def feat_major_out(a_t, lead):
        b_, f_, t_ = a_t.shape
        return a_t.reshape(*lead, b_, f_ // HEAD_DIM, HEAD_DIM, t_).transpose(
            *range(len(lead)), len(lead), len(lead) + 3, len(lead) + 1, len(lead) + 2)

    fox_k_p = feat_major_out(k_t, (1,))
    fox_v_p = feat_major_out(v_t, (1,))
    fox_lf_p = lf_t.transpose(0, 2, 1)[None]
    dil_p = []
    for g, w in enumerate(DIL_WINDOWS):
        wlen = min(w, t)
        sl = kv_tail_t[:, 2 * g * dg:(2 * g + 2) * dg, tail - wlen:]
        dil_p.append(sl.reshape(nb, 2, dg // HEAD_DIM, HEAD_DIM, wlen).transpose(0, 4, 1, 2, 3))

    xs = x_sample.reshape(ns, d)
    (qkv_s,) = _mm_t(wqkv_t, xs[None], (F32,))
    q_st, kn_t, vn_t = qkv_s[0, :d], qkv_s[0, d:2 * d], qkv_s[0, 2 * d:]
    lfn_t, _ = _gate(xs[None], wf_t, wf, b_f)
    lfn_t = lfn_t[0]
    n_phys, ps = cache_fox_k.shape[1], cache_fox_k.shape[2]
    k_pages = cache_fox_k[0].transpose(0, 2, 3, 1).reshape(n_phys, d, ps)
    v_pages = cache_fox_v[0].transpose(0, 2, 3, 1).reshape(n_phys, d, ps)
    lf_pages = cache_fox_logf[0].transpose(0, 2, 1)
    o_t, m_t, l_t, c_t = _fox_decode(page_table, q_st, k_pages, v_pages, lf_pages)
    x1s = _fox_merge(o_t, m_t, l_t, c_t, q_st, kn_t, vn_t, lfn_t, wo_a, xs, lng(0, 0), lnb(0, 0), alpha)
    x2s = moe(x1s, 0)

    (qkv_b,) = _mm_t(wqkv_b_t, x2s[None], (F32,))
    qb_t, kvn_t = qkv_b[0, :dq], qkv_b[0, dq:]
    states, dil_s = [], []
    for g, cache in enumerate((cache_dil_kv_0, cache_dil_kv_1, cache_dil_kv_2)):
        wlen = cache.shape[1]
        buf_t = cache.transpose(0, 2, 3, 4, 1).reshape(ns, 2 * dg, wlen)
        new_buf, og, mg, lg = _dil_decode(buf_t, qb_t[g * dg:(g + 1) * dg], kvn_t[2 * g * dg:(2 * g + 2) * dg], g)
        states.append((og, mg, lg))
        dil_s.append(new_buf.reshape(ns, 2, dg // HEAD_DIM, HEAD_DIM, wlen).transpose(0, 4, 1, 2, 3))
    x3s = _dil_merge(states, qb_t, kvn_t, wo_b, x2s, lng(1, 0), lnb(1, 0), alpha)
    y_sample = moe(x3s, 1).reshape(ns, 1, d)

    fox_k_s = kn_t.reshape(1, 1, nh, HEAD_DIM, ns).transpose(0, 4, 1, 2, 3)
    fox_v_s = vn_t.reshape(1, 1, nh, HEAD_DIM, ns).transpose(0, 4, 1, 2, 3)
    fox_lf_s = lfn_t.reshape(1, 1, nh, ns).transpose(0, 3, 1, 2)

    return (y_prompt, y_sample, fox_k_p, fox_v_p, fox_lf_p, dil_p[0], dil_p[1], dil_p[2],
            fox_k_s, fox_v_s, fox_lf_s, dil_s[0], dil_s[1], dil_s[2])
```

```python
import functools

import jax
import jax.numpy as jnp
from jax import lax
from jax.experimental import pallas as pl
from jax.experimental.pallas import tpu as pltpu

F32 = jnp.float32
BF16 = jnp.bfloat16

HEAD_DIM = 64
LANES = 128
HEADS_PER_SLAB = LANES // HEAD_DIM
DIL_WINDOWS = (128, 512, 2048)
DIL_RATES = (1, 4, 16)
N_DIL_GROUPS = len(DIL_WINDOWS)
DIL_KEYS = 128
N_EXPERT_GROUPS = 4
EXPERTS_PER_GROUP = 8
N_EXPERTS = N_EXPERT_GROUPS * EXPERTS_PER_GROUP
LN_EPS = 1e-5
ATTN_SCALE = HEAD_DIM ** -0.5
LOG2E = 1.4426950408889634
BF16_ROWS = 16
ATTN_UNROLL = 2
NEG = -1e30
VMEM_LIMIT = 48 * 1024 * 1024
DECODE_BLOCK_BYTES = 4 * 1024 * 1024


def _params(*sem):
    return pltpu.CompilerParams(dimension_semantics=sem, vmem_limit_bytes=VMEM_LIMIT)


def _dot(a, b):
    return jnp.dot(a, b, preferred_element_type=F32)


def _dot_nt(a, b):
    return lax.dot_general(a, b, (((1,), (1,)), ((), ())), preferred_element_type=F32)


def _split3(x):
    hi = x.astype(BF16)
    r1 = x - hi.astype(F32)
    mid = r1.astype(BF16)
    lo = (r1 - mid.astype(F32)).astype(BF16)
    return hi, mid, lo


def _layer_norm(z, g, b):
    mu = jnp.mean(z, axis=-1, keepdims=True)
    zc = z - mu
    var = jnp.mean(zc * zc, axis=-1, keepdims=True)
    return zc * lax.rsqrt(var + LN_EPS) * g + b


def _seg_sum(x, seg):
    return jnp.sum(x.reshape(x.shape[0] // seg, seg, x.shape[1]), axis=1)


def _seg_expand(x, seg):
    h, l = x.shape
    return jnp.broadcast_to(x[:, None, :], (h, seg, l)).reshape(h * seg, l)


def _lane_col(x, idx):
    lane = lax.broadcasted_iota(jnp.int32, x.shape, 1)
    return jnp.sum(jnp.where(lane == idx, x, 0.0), axis=1, keepdims=True)


def _put_lane(ref_val, col, idx):
    lane = lax.broadcasted_iota(jnp.int32, ref_val.shape, 1)
    return jnp.where(lane == idx, col, ref_val)


def _upper_tri():
    r = jnp.arange(LANES)
    return (r[:, None] <= r[None, :]).astype(BF16)


def _mm_kernel(x_ref, w_ref, o_ref, *, scale):
    acc = _dot(x_ref[...].astype(BF16), w_ref[...])
    if scale is not None:
        acc = acc * scale
    o_ref[...] = acc.astype(o_ref.dtype)


def _mm(x, w, out_dtype, scale=None, tm=1024, tn=1024):
    n, k = x.shape
    m = w.shape[1]
    tm, tn = min(tm, n), min(tn, m)
    assert n % tm == 0 and m % tn == 0
    return pl.pallas_call(
        functools.partial(_mm_kernel, scale=scale),
        grid=(n // tm, m // tn),
        in_specs=[pl.BlockSpec((tm, k), lambda i, j: (i, 0)),
                  pl.BlockSpec((k, tn), lambda i, j: (0, j))],
        out_specs=pl.BlockSpec((tm, tn), lambda i, j: (i, j)),
        out_shape=jax.ShapeDtypeStruct((n, m), out_dtype),
        compiler_params=_params("parallel", "parallel"),
        name="mm",
    )(x, w)


def _mm_t_kernel(w_ref, x_ref, *o_refs, scale):
    acc = _dot_nt(w_ref[...], x_ref[0].astype(BF16))
    if scale is not None:
        acc = acc * scale
    for o_ref in o_refs:
        o_ref[0] = acc.astype(o_ref.dtype)


def _mm_t(w_t, x, out_dtypes, row_start=0, rows=None, scale=None, tm=1024, tn=1024):
    b, t, k = x.shape
    m = w_t.shape[0]
    rows = t - row_start if rows is None else rows
    tm, tn = min(tm, rows), min(tn, m)
    if m % tn:
        tn //= 2
    assert rows % tm == 0 and m % tn == 0 and row_start % tm == 0
    off = row_start // tm
    return pl.pallas_call(
        functools.partial(_mm_t_kernel, scale=scale),
        grid=(b, rows // tm, m // tn),
        in_specs=[pl.BlockSpec((tn, k), lambda bi, i, j: (j, 0)),
                  pl.BlockSpec((1, tm, k), lambda bi, i, j: (bi, i + off, 0))],
        out_specs=[pl.BlockSpec((1, tn, tm), lambda bi, i, j: (bi, j, i)) for _ in out_dtypes],
        out_shape=[jax.ShapeDtypeStruct((b, m, rows), dt) for dt in out_dtypes],
        compiler_params=_params("parallel", "parallel", "parallel"),
        name="mm_t",
    )(w_t, x)


def _log_sigmoid(y):
    return jnp.minimum(y, 0.0) - jnp.log1p(jnp.exp(-jnp.abs(y)))


def _gate_kernel(x_ref, wft_ref, wf_ref, bfc_ref, bfr_ref, tri_ref, lf_ref, cum_ref, carry_ref):
    @pl.when(pl.program_id(1) == 0)
    def _():
        carry_ref[...] = jnp.zeros_like(carry_ref)

    xb = x_ref[0].astype(BF16)
    lf_ref[0] = _log_sigmoid(_dot_nt(wft_ref[...], xb) + bfc_ref[...])
    lf = _log_sigmoid(_dot(xb, wf_ref[...]) + bfr_ref[...])
    hi, mid, lo = _split3(lf)
    tri = tri_ref[...]
    cs = _dot(tri, hi) + _dot(tri, mid) + _dot(tri, lo) + carry_ref[0:1, :]
    cum_ref[0] = cs * LOG2E
    carry_ref[...] = jnp.broadcast_to(cs[cs.shape[0] - 1:, :], carry_ref.shape)


def _gate(x, wf_t, wf, b_f, tm=512):
    b, t, k = x.shape
    h = wf_t.shape[0]
    tm = min(tm, t)
    r = jnp.arange(tm)
    tri = (r[None, :] <= r[:, None]).astype(BF16)
    fixed = lambda shape: pl.BlockSpec(shape, lambda bi, i: (0, 0))
    return pl.pallas_call(
        _gate_kernel,
        grid=(b, t // tm),
        in_specs=[pl.BlockSpec((1, tm, k), lambda bi, i: (bi, i, 0)),
                  fixed((h, k)), fixed((k, h)), fixed((h, 1)), fixed((1, h)), fixed((tm, tm))],
        out_specs=[pl.BlockSpec((1, h, tm), lambda bi, i: (bi, 0, i)),
                   pl.BlockSpec((1, tm, h), lambda bi, i: (bi, i, 0))],
        out_shape=[jax.ShapeDtypeStruct((b, h, t), F32), jax.ShapeDtypeStruct((b, t, h), F32)],
        scratch_shapes=[pltpu.VMEM((8, h), F32)],
        compiler_params=_params("arbitrary", "arbitrary"),
        name="gate",
    )(x, wf_t, wf, b_f.reshape(h, 1), b_f.reshape(1, h), tri)


def _fox_attn_kernel(qt_ref, k_ref, vt_ref, c_ref, o_ref,
                     s0_ref, s1_ref, p0_ref, p1_ref, al0_ref, al1_ref, acc_ref, m_ref, *, tq):
    iq = pl.program_id(2)
    qt = qt_ref[0]
    row = lax.broadcasted_iota(jnp.int32, qt.shape, 0)
    zero = jnp.zeros_like(qt)
    qtm = [jnp.where(row < HEAD_DIM, qt, zero), jnp.where(row >= HEAD_DIM, qt, zero)]
    ones = jnp.ones((BF16_ROWS, tq), BF16)
    heads = range(HEADS_PER_SLAB)

    s_refs, p_refs, al_refs = (s0_ref, s1_ref), (p0_ref, p1_ref), (al0_ref, al1_ref)

    def scores(j, slot):
        start = pl.multiple_of(j * tq, tq)
        kc = k_ref[0, pl.ds(start, tq), :]
        for a in heads:
            s_refs[slot][a] = _dot(kc, qtm[a]) - c_ref[0, 0, pl.ds(start, tq), a:a + 1]

    def softmax(slot, masked):
        for a in heads:
            st = s_refs[slot][a]
            if masked:
                krow = lax.broadcasted_iota(jnp.int32, st.shape, 0)
                qcol = lax.broadcasted_iota(jnp.int32, st.shape, 1)
                st = jnp.where(krow <= qcol, st, NEG)
            m_old = m_ref[a]
            m_new = jnp.maximum(m_old, jnp.max(st, axis=0, keepdims=True))
            m_ref[a] = m_new
            al_refs[slot][a] = jnp.exp2(m_old - m_new)
            p_refs[slot][a] = jnp.exp2(st - m_new[0:1, :]).astype(BF16)

    def pv(j, slot):
        start = pl.multiple_of(jnp.maximum(j, 0) * tq, tq)
        for a in heads:
            vta = jnp.concatenate([vt_ref[0, a * HEAD_DIM:(a + 1) * HEAD_DIM, pl.ds(start, tq)], ones], axis=0)
            acc_ref[a] = al_refs[slot][a, 0:1, :] * acc_ref[a] + _dot(vta, p_refs[slot][a])

    def stage(j, slot):
        scores(j + 1, 1 - slot)
        softmax(slot, False)
        pv(j - 1, 1 - slot)

    scores(0, 0)
    acc_ref[...] = jnp.zeros_like(acc_ref)
    m_ref[...] = jnp.full_like(m_ref, NEG)
    al1_ref[...] = jnp.ones_like(al1_ref)
    p1_ref[...] = jnp.zeros_like(p1_ref)

    def body(jj, _):
        for u in range(ATTN_UNROLL):
            stage(ATTN_UNROLL * jj + u, u % 2)
        return 0

    lax.fori_loop(0, iq // ATTN_UNROLL, body, 0)
    done = iq - iq % ATTN_UNROLL
    for u in range(ATTN_UNROLL - 1):
        @pl.when(iq % ATTN_UNROLL > u)
        def _():
            stage(done + u, u % 2)

    odd = iq % 2 == 1

    for slot in (0, 1):
        @pl.when(odd == (slot == 1))
        def _():
            softmax(slot, True)
            pv(iq - 1, 1 - slot)
            pv(iq, slot)

    outs = [acc_ref[a, :HEAD_DIM, :] / acc_ref[a, HEAD_DIM:HEAD_DIM + 1, :] for a in heads]
    o_ref[0] = jnp.concatenate(outs, axis=0).astype(o_ref.dtype)


def _fox_attn(q_t, k, v_t, cum_tok, tq=512):
    b, d, t = q_t.shape
    n_slab = d // LANES
    tq = min(tq, t)
    c4 = cum_tok.reshape(b, t, n_slab, HEADS_PER_SLAB).transpose(0, 2, 1, 3)
    return pl.pallas_call(
        functools.partial(_fox_attn_kernel, tq=tq),
        grid=(b, n_slab, t // tq),
        in_specs=[pl.BlockSpec((1, LANES, tq), lambda bi, p, i: (bi, p, i)),
                  pl.BlockSpec((1, t, LANES), lambda bi, p, i: (bi, 0, p)),
                  pl.BlockSpec((1, LANES, t), lambda bi, p, i: (bi, p, 0)),
                  pl.BlockSpec((1, 1, t, HEADS_PER_SLAB), lambda bi, p, i: (bi, p, 0, 0))],
        out_specs=pl.BlockSpec((1, LANES, tq), lambda bi, p, i: (bi, p, i)),
        out_shape=jax.ShapeDtypeStruct((b, d, t), BF16),
        scratch_shapes=([pltpu.VMEM((HEADS_PER_SLAB, tq, tq), F32)] * 2
                        + [pltpu.VMEM((HEADS_PER_SLAB, tq, tq), BF16)] * 2
                        + [pltpu.VMEM((HEADS_PER_SLAB, 8, tq), F32)] * 2
                        + [pltpu.VMEM((HEADS_PER_SLAB, HEAD_DIM + BF16_ROWS, tq), F32),
                           pltpu.VMEM((HEADS_PER_SLAB, 8, tq), F32)]),
        compiler_params=_params("parallel", "parallel", "arbitrary"),
        name="fox_attn",
    )(q_t, k, v_t, c4)


def _proj_norm_t_kernel(at_ref, w_ref, x_ref, g_ref, b_ref, o_ref, *, alpha):
    mix = lax.dot_general(at_ref[0], w_ref[...], (((0,), (0,)), ((), ())), preferred_element_type=F32)
    o_ref[0] = _layer_norm(alpha * x_ref[0] + mix, g_ref[...], b_ref[...])


def _proj_norm_t(a_t, w, x, g, b, alpha, tm=1024):
    nb, ka, t = a_t.shape
    d = w.shape[1]
    tm = min(tm, t)
    fixed = lambda shape: pl.BlockSpec(shape, lambda bi, i: (0, 0))
    return pl.pallas_call(
        functools.partial(_proj_norm_t_kernel, alpha=alpha),
        grid=(nb, t // tm),
        in_specs=[pl.BlockSpec((1, ka, tm), lambda bi, i: (bi, 0, i)),
                  fixed((ka, d)),
                  pl.BlockSpec((1, tm, d), lambda bi, i: (bi, i, 0)),
                  fixed((1, d)), fixed((1, d))],
        out_specs=pl.BlockSpec((1, tm, d), lambda bi, i: (bi, i, 0)),
        out_shape=jax.ShapeDtypeStruct((nb, t, d), F32),
        compiler_params=_params("parallel", "parallel"),
        name="proj_norm_t",
    )(a_t, w, x, g, b)


def _route(logits):
    lane = lax.broadcasted_iota(jnp.int32, logits.shape, 1).astype(F32)
    far = float(LANES)
    gl = jnp.where(lane < N_EXPERT_GROUPS, logits, NEG)
    gmax = jnp.max(gl, axis=1, keepdims=True)
    gsel = jnp.min(jnp.where(gl == gmax, lane, far), axis=1, keepdims=True)
    psel = 1.0 / jnp.sum(jnp.exp(gl - gmax), axis=1, keepdims=True)
    lo = N_EXPERT_GROUPS + gsel * EXPERTS_PER_GROUP
    el = jnp.where((lane >= lo) & (lane < lo + EXPERTS_PER_GROUP), logits, NEG)
    v1 = jnp.max(el, axis=1, keepdims=True)
    i1 = jnp.min(jnp.where(el == v1, lane, far), axis=1, keepdims=True)
    el2 = jnp.where(lane == i1, NEG, el)
    v2 = jnp.max(el2, axis=1, keepdims=True)
    i2 = jnp.min(jnp.where(el2 == v2, lane, far), axis=1, keepdims=True)
    e2 = jnp.exp(v2 - v1)
    den = 1.0 + e2
    return jnp.where(lane == i1, psel / den, jnp.where(lane == i2, psel * e2 / den, 0.0))


def _moe_kernel(x_ref, wr_ref, br_ref, wg_ref, wu_ref, wd_ref, g_ref, b_ref, o_ref,
                xb_ref, gates_ref, acc_ref, *, alpha):
    e = pl.program_id(1)

    @pl.when(e == 0)
    def _():
        x = x_ref[...]
        xh = x.astype(BF16)
        xl = (x - xh.astype(F32)).astype(BF16)
        wr = wr_ref[...]
        wh = wr.astype(BF16)
        wl = (wr - wh.astype(F32)).astype(BF16)
        xb_ref[...] = xh
        logits = _dot(xh, wh) + _dot(xl, wh) + _dot(xh, wl) + br_ref[...]
        gates_ref[...] = _route(logits)
        acc_ref[...] = jnp.zeros_like(acc_ref)

    xb = xb_ref[...]
    hg = _dot(xb, wg_ref[0].astype(BF16))
    hu = _dot(xb, wu_ref[0].astype(BF16))
    gcol = _lane_col(gates_ref[...], e + N_EXPERT_GROUPS)
    h = hg * (1.0 / (1.0 + jnp.exp(-hg))) * hu
    acc_ref[...] += _dot((h * gcol).astype(BF16), wd_ref[0].astype(BF16))

    @pl.when(e == pl.num_programs(1) - 1)
    def _():
        o_ref[...] = _layer_norm(alpha * x_ref[...] + acc_ref[...], g_ref[...], b_ref[...])


def _moe_norm(x, wr, br, wg, wu, wd, g, b, alpha, tm=1024):
    n, d = x.shape
    ne, _, f = wg.shape
    tm = min(tm, n)
    return pl.pallas_call(
        functools.partial(_moe_kernel, alpha=alpha),
        grid=(n // tm, ne),
        in_specs=[pl.BlockSpec((tm, d), lambda i, e: (i, 0)),
                  pl.BlockSpec((d, LANES), lambda i, e: (0, 0)),
                  pl.BlockSpec((1, LANES), lambda i, e: (0, 0)),
                  pl.BlockSpec((1, d, f), lambda i, e: (e, 0, 0)),
                  pl.BlockSpec((1, d, f), lambda i, e: (e, 0, 0)),
                  pl.BlockSpec((1, f, d), lambda i, e: (e, 0, 0)),
                  pl.BlockSpec((1, d), lambda i, e: (0, 0)),
                  pl.BlockSpec((1, d), lambda i, e: (0, 0))],
        out_specs=pl.BlockSpec((tm, d), lambda i, e: (i, 0)),
        out_shape=jax.ShapeDtypeStruct((n, d), F32),
        scratch_shapes=[pltpu.VMEM((tm, d), BF16), pltpu.VMEM((tm, LANES), F32), pltpu.VMEM((tm, d), F32)],
        compiler_params=_params("parallel", "arbitrary"),
        name="moe_norm",
    )(x, wr, br, wg, wu, wd, g, b)


def _route_t_kernel(x_ref, wr_ref, br_ref, u_ref, route_ref, cnt_ref, carry_ref):
    @pl.when(pl.program_id(0) == 0)
    def _():
        carry_ref[...] = jnp.zeros_like(carry_ref)

    x = x_ref[...]
    xh = x.astype(BF16)
    xl = (x - xh.astype(F32)).astype(BF16)
    w = wr_ref[...]
    wh = w.astype(BF16)
    wl = (w - wh.astype(F32)).astype(BF16)
    logits = _dot_nt(wh, xh) + _dot_nt(wh, xl) + _dot_nt(wl, xh) + br_ref[...]
    r = lax.broadcasted_iota(jnp.int32, logits.shape, 0).astype(F32)
    far = float(logits.shape[0])
    gl = jnp.where(r < N_EXPERT_GROUPS, logits, NEG)
    gmax = jnp.max(gl, axis=0, keepdims=True)
    gsel = jnp.min(jnp.where(gl == gmax, r, far), axis=0, keepdims=True)
    psel = 1.0 / jnp.sum(jnp.exp(gl - gmax), axis=0, keepdims=True)
    lo = N_EXPERT_GROUPS + gsel * EXPERTS_PER_GROUP
    el = jnp.where((r >= lo) & (r < lo + EXPERTS_PER_GROUP), logits, NEG)
    v1 = jnp.max(el, axis=0, keepdims=True)
    i1 = jnp.min(jnp.where(el == v1, r, far), axis=0, keepdims=True)
    el2 = jnp.where(r == i1, NEG, el)
    v2 = jnp.max(el2, axis=0, keepdims=True)
    i2 = jnp.min(jnp.where(el2 == v2, r, far), axis=0, keepdims=True)
    e2 = jnp.exp(v2 - v1)
    den = 1.0 + e2
    onehot = jnp.where(r == i1, 1.0, jnp.where(r == i2, 1.0, 0.0)).astype(BF16)
    cum = _dot(onehot, u_ref[...]) + carry_ref[:, 0:1]
    rank1 = jnp.sum(jnp.where(r == i1, cum, 0.0), axis=0, keepdims=True) - 1.0
    rank2 = jnp.sum(jnp.where(r == i2, cum, 0.0), axis=0, keepdims=True) - 1.0
    carry_ref[...] = jnp.broadcast_to(cum[:, cum.shape[1] - 1:], carry_ref.shape)
    cnt_ref[...] = carry_ref[...]
    rows = (i1 - N_EXPERT_GROUPS, i2 - N_EXPERT_GROUPS, psel / den, psel * e2 / den, rank1, rank2)
    r8 = lax.broadcasted_iota(jnp.int32, route_ref.shape, 0)
    out = jnp.zeros(route_ref.shape, F32)
    for k, v in enumerate(rows):
        out = jnp.where(r8 == k, v, out)
    route_ref[...] = out


def _route_t(x, wr_t, br_t, tm=512):
    n, d = x.shape
    nr = wr_t.shape[0]
    tm = min(tm, n)
    r = jnp.arange(tm)
    upper = (r[:, None] <= r[None, :]).astype(BF16)
    fixed = lambda shape: pl.BlockSpec(shape, lambda i: (0, 0))
    return pl.pallas_call(
        _route_t_kernel,
        grid=(n // tm,),
        in_specs=[pl.BlockSpec((tm, d), lambda i: (i, 0)), fixed((nr, d)), fixed((nr, 1)), fixed((tm, tm))],
        out_specs=[pl.BlockSpec((8, tm), lambda i: (0, i)), fixed((nr, LANES))],
        out_shape=[jax.ShapeDtypeStruct((8, n), F32), jax.ShapeDtypeStruct((nr, LANES), F32)],
        scratch_shapes=[pltpu.VMEM((nr, LANES), F32)],
        compiler_params=_params("arbitrary"),
        name="route",
    )(x, wr_t, br_t, upper)


def _positions_kernel(route_ref, off_ref, pos_ref):
    rt = route_ref[...]
    off = off_ref[...]
    r = lax.broadcasted_iota(jnp.int32, (off.shape[0], rt.shape[1]), 0).astype(F32) - float(N_EXPERT_GROUPS)
    rows = [jnp.sum(jnp.where(r == rt[k:k + 1], off, 0.0), axis=0, keepdims=True) + rt[4 + k:5 + k]
            for k in range(2)]
    pos_ref[0] = jnp.concatenate(rows, axis=0).astype(jnp.int32)


def _positions(route, offsets, tt):
    n = route.shape[1]
    return pl.pallas_call(
        _positions_kernel,
        grid=(n // tt,),
        in_specs=[pl.BlockSpec((8, tt), lambda i: (0, i)), pl.BlockSpec(offsets.shape, lambda i: (0, 0))],
        out_specs=pl.BlockSpec((1, 2, tt), lambda i: (i, 0, 0)),
        out_shape=jax.ShapeDtypeStruct((n // tt, 2, tt), jnp.int32),
        compiler_params=_params("parallel"),
        name="positions",
    )(route, offsets)


def _row_copies(idx_ref, tt, src_of, dst_of, sem):
    def each(fn):
        def body(j, carry):
            for k in range(2):
                p = idx_ref[k * tt + j]
                fn(pltpu.make_async_copy(src_of(k, j, p), dst_of(k, j, p), sem))
            return carry
        lax.fori_loop(0, tt, body, 0, unroll=8)
    each(lambda cp: cp.start())
    return lambda: each(lambda cp: cp.wait())


def _dispatch_kernel(pos_ref, x_ref, xs_in_ref, xs_ref, idx_ref, sem_i, sem_r, *, tt):
    del xs_in_ref
    i = pl.program_id(0)
    cp = pltpu.make_async_copy(pos_ref.at[pl.ds(i * 2 * tt, 2 * tt)], idx_ref, sem_i)
    cp.start()
    cp.wait()
    wait = _row_copies(idx_ref, tt,
                       lambda k, j, p: x_ref.at[pl.ds(j, 1), :],
                       lambda k, j, p: xs_ref.at[pl.ds(p, 1), :], sem_r)
    wait()


def _dispatch(pos_flat, x, n_rows, tt):
    n, d = x.shape
    return pl.pallas_call(
        functools.partial(_dispatch_kernel, tt=tt),
        grid=(n // tt,),
        in_specs=[pl.BlockSpec(memory_space=pl.ANY), pl.BlockSpec((tt, d), lambda i: (i, 0)),
                  pl.BlockSpec(memory_space=pl.ANY)],
        out_specs=pl.BlockSpec(memory_space=pl.ANY),
        out_shape=jax.ShapeDtypeStruct((n_rows, d), F32),
        scratch_shapes=[pltpu.SMEM((2 * tt,), jnp.int32), pltpu.SemaphoreType.DMA(()), pltpu.SemaphoreType.DMA(())],
        input_output_aliases={2: 0},
        compiler_params=_params("arbitrary"),
        name="dispatch",
    )(pos_flat, x, jnp.zeros((n_rows, d), F32))


def _expert_kernel(te_ref, blk_ref, nv_ref, xs_ref, wg_ref, wu_ref, wd_ref, ys_ref, wgb, wub, wdb):
    i = pl.program_id(0)

    @pl.when((i == 0) | (te_ref[i] != te_ref[jnp.maximum(i - 1, 0)]))
    def _():
        wgb[...] = wg_ref[0].astype(BF16)
        wub[...] = wu_ref[0].astype(BF16)
        wdb[...] = wd_ref[0].astype(BF16)

    @pl.when(i < nv_ref[0])
    def _():
        xb = xs_ref[...].astype(BF16)
        hg = _dot(xb, wgb[...])
        hu = _dot(xb, wub[...])
        h = hg * (1.0 / (1.0 + jnp.exp(-hg))) * hu
        ys_ref[...] = _dot(h.astype(BF16), wdb[...])

    @pl.when(i >= nv_ref[0])
    def _():
        ys_ref[...] = jnp.zeros_like(ys_ref)


def _experts(tile_expert, tile_block, n_valid, xs, wg, wu, wd, rows):
    n_rows, d = xs.shape
    f = wg.shape[2]
    n_tiles = tile_expert.shape[0]
    row_spec = pl.BlockSpec((rows, d), lambda i, te, blk, nv: (blk[i], 0))
    grid_spec = pltpu.PrefetchScalarGridSpec(
        num_scalar_prefetch=3,
        grid=(n_tiles,),
        in_specs=[row_spec,
                  pl.BlockSpec((1, d, f), lambda i, te, blk, nv: (te[i], 0, 0)),
                  pl.BlockSpec((1, d, f), lambda i, te, blk, nv: (te[i], 0, 0)),
                  pl.BlockSpec((1, f, d), lambda i, te, blk, nv: (te[i], 0, 0))],
        out_specs=row_spec,
        scratch_shapes=[pltpu.VMEM((d, f), BF16), pltpu.VMEM((d, f), BF16), pltpu.VMEM((f, d), BF16)],
    )
    return pl.pallas_call(
        _expert_kernel,
        grid_spec=grid_spec,
        out_shape=jax.ShapeDtypeStruct((n_rows, d), F32),
        compiler_params=_params("arbitrary"),
        name="experts",
    )(tile_expert, tile_block, n_valid, xs, wg, wu, wd)


def _combine_kernel(pos_ref, ys_ref, x_ref, gs_ref, sel_ref, g_ref, b_ref, o_ref, idx_ref, ybuf, sem_i, sem_r,
                    *, tt, alpha):
    i = pl.program_id(0)
    cp = pltpu.make_async_copy(pos_ref.at[pl.ds(i * 2 * tt, 2 * tt)], idx_ref, sem_i)
    cp.start()
    cp.wait()
    wait = _row_copies(idx_ref, tt,
                       lambda k, j, p: ys_ref.at[pl.ds(p, 1), :],
                       lambda k, j, p: ybuf.at[k, pl.ds(j, 1), :], sem_r)
    wcols = lax.dot_general(gs_ref[...], sel_ref[...], (((0,), (0,)), ((), ())), preferred_element_type=F32)
    w1, w2 = wcols[:, 0:1], wcols[:, LANES:LANES + 1]
    wait()
    z = alpha * x_ref[...] + w1 * ybuf[0] + w2 * ybuf[1]
    o_ref[...] = _layer_norm(z, g_ref[...], b_ref[...])


def _combine(pos_flat, ys, x, gate_split, g, b, alpha, tt):
    n, d = x.shape
    sel = jnp.zeros((8, 2 * LANES), BF16).at[0:2, :LANES].set(1).at[2:4, LANES:].set(1)
    fixed = lambda shape: pl.BlockSpec(shape, lambda i: (0, 0))
    return pl.pallas_call(
        functools.partial(_combine_kernel, tt=tt, alpha=alpha),
        grid=(n // tt,),
        in_specs=[pl.BlockSpec(memory_space=pl.ANY), pl.BlockSpec(memory_space=pl.ANY),
                  pl.BlockSpec((tt, d), lambda i: (i, 0)),
                  pl.BlockSpec((8, tt), lambda i: (0, i)),
                  fixed((8, 2 * LANES)), fixed((1, d)), fixed((1, d))],
        out_specs=pl.BlockSpec((tt, d), lambda i: (i, 0)),
        out_shape=jax.ShapeDtypeStruct((n, d), F32),
        scratch_shapes=[pltpu.SMEM((2 * tt,), jnp.int32), pltpu.VMEM((2, tt, d), F32),
                        pltpu.SemaphoreType.DMA(()), pltpu.SemaphoreType.DMA(())],
        compiler_params=_params("arbitrary"),
        name="combine",
    )(pos_flat, ys, x, gate_split, sel, g, b)


def _moe_routed_norm(x, wr_t, br_t, wg, wu, wd, g, b, alpha, rows=256, tt=512):
    n, d = x.shape
    ne = wg.shape[0]
    route, counts = _route_t(x, wr_t, br_t)
    cnt = counts[N_EXPERT_GROUPS:N_EXPERT_GROUPS + ne, 0].astype(jnp.int32)
    tiles = (cnt + rows - 1) // rows
    tile_end = jnp.cumsum(tiles)
    offsets = jnp.zeros((counts.shape[0], 1), F32).at[N_EXPERT_GROUPS:N_EXPERT_GROUPS + ne, 0].set(
        ((tile_end - tiles) * rows).astype(F32))
    pos_flat = _positions(route, offsets, tt).reshape(-1)
    max_tiles = 2 * n // rows + ne
    n_valid = tile_end[ne - 1]
    tile_block = jnp.arange(max_tiles, dtype=jnp.int32)
    last = jnp.maximum(n_valid - 1, 0)
    tile_expert = jnp.minimum(jnp.sum(tile_end[None, :] <= jnp.minimum(tile_block, last)[:, None], axis=1),
                              ne - 1).astype(jnp.int32)
    gate = route[2:4]
    g_hi = gate.astype(BF16)
    g_lo = (gate - g_hi.astype(F32)).astype(BF16)
    gate_split = jnp.concatenate([g_hi[0:1], g_lo[0:1], g_hi[1:2], g_lo[1:2], jnp.zeros((4, n), BF16)], axis=0)

    xs = _dispatch(pos_flat, x, max_tiles * rows, tt)
    ys = _experts(tile_expert, tile_block, n_valid.reshape(1).astype(jnp.int32), xs, wg, wu, wd, rows)
    return _combine(pos_flat, ys, x, gate_split, g, b, alpha, tt)


def _alibi_slope(g, h, n_heads):
    return 2.0 ** (-8.0 * (h * N_DIL_GROUPS + g + 1) / (N_DIL_GROUPS * n_heads))


def _mm_dil_kernel(x_ref, w_ref, o_ref, acc_ref, *, rate):
    acc = _dot(x_ref[0].astype(BF16), w_ref[...])
    if rate == 1:
        o_ref[0, 0] = acc.astype(o_ref.dtype)
        return
    rows = acc.shape[0] // rate
    for c in range(acc_ref.shape[0]):
        acc_ref[c] = acc[:, c * LANES:(c + 1) * LANES]
    for r in range(rate):
        for c in range(acc_ref.shape[0]):
            o_ref[0, r, :, c * LANES:(c + 1) * LANES] = \
                acc_ref[c, pl.ds(r, rows, stride=rate), :].astype(o_ref.dtype)


def _mm_dil(x, w, rate, tm=1024, tn=1536):
    b, t, k = x.shape
    m = w.shape[1]
    tm, tn = min(tm, t), min(tn, m)
    assert t % tm == 0 and m % tn == 0 and tm % rate == 0
    return pl.pallas_call(
        functools.partial(_mm_dil_kernel, rate=rate),
        grid=(b, t // tm, m // tn),
        in_specs=[pl.BlockSpec((1, tm, k), lambda bi, i, j: (bi, i, 0)),
                  pl.BlockSpec((k, tn), lambda bi, i, j: (0, j))],
        out_specs=pl.BlockSpec((1, rate, tm // rate, tn), lambda bi, i, j: (bi, 0, i, j)),
        out_shape=jax.ShapeDtypeStruct((b, rate, t // rate, m), BF16),
        scratch_shapes=[pltpu.VMEM((tn // LANES, tm, LANES), F32)],
        compiler_params=_params("parallel", "parallel", "parallel"),
        name=f"mm_dil{rate}",
    )(x, w)


def _dil_attn_kernel(q_ref, kc_ref, kp_ref, vc_ref, vp_ref, o_ref, lse_ref, kbuf, vbuf, *, tu, rate, slopes):
    i = pl.program_id(2)
    kbuf[0:DIL_KEYS] = kp_ref[0, 0]
    kbuf[DIL_KEYS:] = kc_ref[0, 0]
    vbuf[0:DIL_KEYS] = vp_ref[0, 0]
    vbuf[DIL_KEYS:] = vc_ref[0, 0]
    shape = (DIL_KEYS, 2 * DIL_KEYS)
    a = lax.broadcasted_iota(jnp.int32, shape, 0)
    c = lax.broadcasted_iota(jnp.int32, shape, 1)
    dist = a - c + DIL_KEYS
    base = jnp.where((dist >= 0) & (dist <= DIL_KEYS), (-rate * dist).astype(F32), NEG)
    base_first = jnp.where(c >= DIL_KEYS, base, NEG)
    lane = lax.broadcasted_iota(jnp.int32, (DIL_KEYS, LANES), 1)
    left = lane < HEAD_DIM
    for s in range(tu // DIL_KEYS):
        bias = jnp.where(i == 0, base_first, base) if s == 0 else base
        r0 = s * DIL_KEYS
        for p in range(q_ref.shape[3] // LANES):
            cs = slice(p * LANES, (p + 1) * LANES)
            qp = q_ref[0, 0, r0:r0 + DIL_KEYS, cs]
            kk = kbuf[r0:r0 + 2 * DIL_KEYS, cs]
            vv = vbuf[r0:r0 + 2 * DIL_KEYS, cs]
            outs, lses = [], []
            for hh in range(HEADS_PER_SLAB):
                qm = jnp.where(left if hh == 0 else jnp.logical_not(left), qp, jnp.zeros_like(qp))
                sc = _dot_nt(qm, kk) + slopes[HEADS_PER_SLAB * p + hh] * bias
                m = jnp.max(sc, axis=1, keepdims=True)
                e = jnp.exp(sc - m)
                l = jnp.sum(e, axis=1, keepdims=True)
                outs.append(_dot(e.astype(BF16), vv) / l)
                lses.append(m + jnp.log(l))
            o_ref[0, 0, r0:r0 + DIL_KEYS, cs] = jnp.where(left, outs[0], outs[1])
            lse_ref[0, 0, r0:r0 + DIL_KEYS, cs] = jnp.where(left, lses[0], lses[1])


def _dil_attn(qkv, g, tu=512):
    b, rate, u, dg3 = qkv.shape
    dg = dg3 // 3
    tu = min(tu, u)
    ratio = tu // DIL_KEYS
    slopes = tuple(_alibi_slope(g, h, dg // HEAD_DIM) for h in range(dg // HEAD_DIM))
    cur = lambda col: pl.BlockSpec((1, 1, tu, dg), lambda bi, r, i: (bi, r, i, col))
    prev = lambda col: pl.BlockSpec((1, 1, DIL_KEYS, dg),
                                    lambda bi, r, i: (bi, r, jnp.maximum(i * ratio - 1, 0), col))
    return pl.pallas_call(
        functools.partial(_dil_attn_kernel, tu=tu, rate=rate, slopes=slopes),
        grid=(b, rate, u // tu),
        in_specs=[cur(0), cur(1), prev(1), cur(2), prev(2)],
        out_specs=[cur(0)] * 2,
        out_shape=[jax.ShapeDtypeStruct((b, rate, u, dg), F32)] * 2,
        scratch_shapes=[pltpu.VMEM((tu + DIL_KEYS, dg), BF16)] * 2,
        compiler_params=_params("parallel", "parallel", "arbitrary"),
        name=f"dil_attn{g}",
    )(qkv, qkv, qkv, qkv, qkv)


def _mix_norm_kernel(*refs, alpha, rates):
    ng = len(rates)
    o_refs, l_refs = refs[:ng], refs[ng:2 * ng]
    w_ref, x_ref, g_ref, b_ref, out_ref = refs[2 * ng:2 * ng + 5]
    bufs = refs[2 * ng + 5:]

    def natural(ref, buf, rate):
        if rate == 1:
            return ref[0, 0]
        for r in range(rate):
            for c in range(buf.shape[0]):
                buf[c, pl.ds(r, ref.shape[2], stride=rate), :] = ref[0, r, :, c * LANES:(c + 1) * LANES]
        return jnp.concatenate([buf[c] for c in range(buf.shape[0])], axis=1)

    os_ = [natural(o_refs[g], bufs[g], rates[g]) for g in range(ng)]
    ls = [natural(l_refs[g], bufs[ng + g], rates[g]) for g in range(ng)]
    m = functools.reduce(jnp.maximum, ls)
    es = [jnp.exp(l - m) for l in ls]
    o = sum(e * o_ for e, o_ in zip(es, os_)) / sum(es)
    mix = _dot(o.astype(BF16), w_ref[...])
    out_ref[0] = _layer_norm(alpha * x_ref[0] + mix, g_ref[...], b_ref[...])


def _mix_norm(os_, ls, w, x, g, b, alpha, tm=512):
    nb, t, d = x.shape
    dg = w.shape[0]
    tm = min(tm, t)
    rates = tuple(o.shape[1] for o in os_)
    grp = lambda rate: pl.BlockSpec((1, rate, tm // rate, dg), lambda bi, i: (bi, 0, i, 0))
    fixed = lambda shape: pl.BlockSpec(shape, lambda bi, i: (0, 0))
    xrow = pl.BlockSpec((1, tm, d), lambda bi, i: (bi, i, 0))
    return pl.pallas_call(
        functools.partial(_mix_norm_kernel, alpha=alpha, rates=rates),
        grid=(nb, t // tm),
        in_specs=[grp(r) for r in rates] * 2 + [fixed((dg, d)), xrow, fixed((1, d)), fixed((1, d))],
        out_specs=xrow,
        out_shape=jax.ShapeDtypeStruct((nb, t, d), F32),
        scratch_shapes=[pltpu.VMEM((dg // LANES, tm, LANES), F32)] * (2 * len(rates)),
        compiler_params=_params("parallel", "parallel"),
        name="mix_norm",
    )(*os_, *ls, w, x, g, b)


def _fox_decode_kernel(pt_ref, qt_ref, u_ref, *refs, pages):
    k_refs, v_refs, lf_refs = refs[:pages], refs[pages:2 * pages], refs[2 * pages:3 * pages]
    ot_ref, mt_ref, lt_ref, ct_ref, qcol_ref, acc_ref, m_ref, l_ref, c_ref = refs[3 * pages:]
    b, j = pl.program_id(0), pl.program_id(1)
    nh = m_ref.shape[0]

    @pl.when((b == 0) & (j == 0))
    def _():
        for r in (ot_ref, mt_ref, lt_ref, ct_ref):
            r[...] = jnp.zeros_like(r)

    @pl.when(j == 0)
    def _():
        qcol = _lane_col(qt_ref[...], b) * ATTN_SCALE
        qcol_ref[...] = jnp.broadcast_to(qcol.astype(BF16).astype(F32), qcol_ref.shape)
        acc_ref[...] = jnp.zeros_like(acc_ref)
        m_ref[...] = jnp.full_like(m_ref, NEG)
        l_ref[...] = jnp.zeros_like(l_ref)
        c_ref[...] = jnp.zeros_like(c_ref)

    u = u_ref[...]
    for pi in range(pages):
        hi, mid, lo = _split3(lf_refs[pi][0])
        cs = _dot(hi, u) + _dot(mid, u) + _dot(lo, u) + c_ref[...]
        c_ref[...] = jnp.broadcast_to(cs[:, LANES - 1:LANES], c_ref.shape)
        s = _seg_sum(qcol_ref[...] * k_refs[pi][0], HEAD_DIM) - cs
        m_old = m_ref[...]
        m_new = jnp.maximum(m_old, jnp.max(s, axis=1, keepdims=True))
        alpha = jnp.exp(m_old - m_new)
        p = jnp.exp(s - m_new)
        m_ref[...] = m_new
        l_ref[...] = alpha * l_ref[...] + p
        acc_ref[...] = (acc_ref[...] * _seg_expand(alpha, HEAD_DIM)
                        + _seg_expand(p, HEAD_DIM) * v_refs[pi][0])

    @pl.when(j == pl.num_programs(1) - 1)
    def _():
        ot_ref[...] = _put_lane(ot_ref[...], jnp.sum(acc_ref[...], axis=1, keepdims=True), b)
        lt_ref[...] = _put_lane(lt_ref[...], jnp.sum(l_ref[...], axis=1, keepdims=True), b)
        mt_ref[...] = _put_lane(mt_ref[...], m_ref[:, 0:1], b)
        ct_ref[...] = _put_lane(ct_ref[...], c_ref[:, 0:1], b)


def _fox_decode(page_table, q_t, k_pages, v_pages, lf_pages, pages=8):
    d, nb = q_t.shape
    nh = lf_pages.shape[1]
    n_pages = page_table.shape[1]
    ps = k_pages.shape[2]
    pt = page_table.reshape(-1)

    def page_spec(rows, pi):
        return pl.BlockSpec((1, rows, ps), lambda b, j, pt_ref: (pt_ref[b * n_pages + j * pages + pi], 0, 0))

    fixed = lambda shape: pl.BlockSpec(shape, lambda b, j, pt_ref: (0, 0))
    grid_spec = pltpu.PrefetchScalarGridSpec(
        num_scalar_prefetch=1,
        grid=(nb, n_pages // pages),
        in_specs=([fixed((d, nb)), fixed((LANES, LANES))]
                  + [page_spec(d, pi) for pi in range(pages)] * 2
                  + [page_spec(nh, pi) for pi in range(pages)]),
        out_specs=[fixed((d, nb)), fixed((nh, nb)), fixed((nh, nb)), fixed((nh, nb))],
        scratch_shapes=[pltpu.VMEM((d, ps), F32), pltpu.VMEM((d, ps), F32),
                        pltpu.VMEM((nh, ps), F32), pltpu.VMEM((nh, ps), F32), pltpu.VMEM((nh, ps), F32)],
    )
    return pl.pallas_call(
        functools.partial(_fox_decode_kernel, pages=pages),
        grid_spec=grid_spec,
        out_shape=[jax.ShapeDtypeStruct((d, nb), F32)] + [jax.ShapeDtypeStruct((nh, nb), F32)] * 3,
        compiler_params=_params("arbitrary", "arbitrary"),
        name="fox_decode",
    )(pt, q_t, _upper_tri(), *([k_pages] * pages), *([v_pages] * pages), *([lf_pages] * pages))


def _fox_merge_kernel(ot_ref, mt_ref, lt_ref, ct_ref, qt_ref, kn_ref, vn_ref, lfn_ref,
                      w_ref, x_ref, g_ref, b_ref, out_ref, *, alpha):
    qs = (qt_ref[...] * ATTN_SCALE).astype(BF16).astype(F32)
    s_new = _seg_sum(qs * kn_ref[...], HEAD_DIM) - (ct_ref[...] + lfn_ref[...])
    m_p = mt_ref[...]
    m = jnp.maximum(m_p, s_new)
    a_p = jnp.exp(m_p - m)
    a_n = jnp.exp(s_new - m)
    l = lt_ref[...] * a_p + a_n
    o_t = (ot_ref[...] * _seg_expand(a_p, HEAD_DIM) + _seg_expand(a_n, HEAD_DIM) * vn_ref[...]) \
        / _seg_expand(l, HEAD_DIM)
    mix = _dot(o_t.T.astype(BF16), w_ref[...])
    out_ref[...] = _layer_norm(alpha * x_ref[...] + mix, g_ref[...], b_ref[...])


def _fox_merge(o_t, m_t, l_t, c_t, q_t, kn_t, vn_t, lfn_t, w, x, g, b, alpha):
    return pl.pallas_call(
        functools.partial(_fox_merge_kernel, alpha=alpha),
        out_shape=jax.ShapeDtypeStruct(x.shape, F32),
        compiler_params=pltpu.CompilerParams(vmem_limit_bytes=VMEM_LIMIT),
        name="fox_merge",
    )(o_t, m_t, l_t, c_t, q_t, kn_t, vn_t, lfn_t, w, x, g, b)


def _dil_decode_kernel(buf_ref, qt_ref, kvn_ref, sl_ref, out_ref, ot_ref, mt_ref, lt_ref, p_ref,
                       *, window, rate):
    bi, c = pl.program_id(0), pl.program_id(1)

    @pl.when((bi == 0) & (c == 0))
    def _():
        for r in (ot_ref, mt_ref, lt_ref):
            r[...] = jnp.zeros_like(r)

    t = lax.broadcasted_iota(jnp.int32, (1, window), 1)
    for bb in range(buf_ref.shape[0]):
        b = bi * buf_ref.shape[0] + bb
        x = buf_ref[bb]

        @pl.when(c == 0)
        def _():
            qcol = (_lane_col(qt_ref[...], b) * ATTN_SCALE).astype(BF16).astype(F32)
            s = _seg_sum(x * qcol, HEAD_DIM)
            dist = (window - t).astype(F32)
            s = jnp.where((t & (rate - 1)) == 0, s - sl_ref[:, 0:1] * dist, NEG)
            m = jnp.max(s, axis=1, keepdims=True)
            p = jnp.exp(s - m)
            p_ref[bb] = p
            mt_ref[...] = _put_lane(mt_ref[...], m, b)
            lt_ref[...] = _put_lane(lt_ref[...], jnp.sum(p, axis=1, keepdims=True), b)

        @pl.when(c == 1)
        def _():
            o = jnp.sum(x * _seg_expand(p_ref[bb], HEAD_DIM), axis=1, keepdims=True)
            ot_ref[...] = _put_lane(ot_ref[...], o, b)

        newcol = _lane_col(kvn_ref[...], b)
        shifted = pltpu.roll(x, window - 1, axis=1)
        out_ref[bb] = jnp.where(t == window - 1, newcol, shifted)


def _dil_decode(buf_t, qg_t, kvn_t, g):
    nb, _, window = buf_t.shape
    dg = qg_t.shape[0]
    nh = dg // HEAD_DIM
    rate = DIL_RATES[g]
    assert rate & (rate - 1) == 0 and window == DIL_KEYS * rate
    slopes = jnp.asarray([_alibi_slope(g, h, nh) for h in range(nh)], F32)
    sl = jnp.broadcast_to(slopes.reshape(nh, 1), (nh, LANES))
    fixed = lambda shape: pl.BlockSpec(shape, lambda b, c: (0, 0))
    per_step = max(1, min(nb, DECODE_BLOCK_BYTES // (dg * window * 4)))
    assert nb % per_step == 0
    return pl.pallas_call(
        functools.partial(_dil_decode_kernel, window=window, rate=rate),
        grid=(nb // per_step, 2),
        in_specs=[pl.BlockSpec((per_step, dg, window), lambda b, c: (b, c, 0)),
                  fixed((dg, nb)),
                  pl.BlockSpec((dg, nb), lambda b, c: (c, 0)),
                  fixed((nh, LANES))],
        out_specs=[pl.BlockSpec((per_step, dg, window), lambda b, c: (b, c, 0)),
                   fixed((dg, nb)), fixed((nh, nb)), fixed((nh, nb))],
        out_shape=[jax.ShapeDtypeStruct(buf_t.shape, F32),
                   jax.ShapeDtypeStruct((dg, nb), F32),
                   jax.ShapeDtypeStruct((nh, nb), F32),
                   jax.ShapeDtypeStruct((nh, nb), F32)],
        scratch_shapes=[pltpu.VMEM((per_step, nh, window), F32)],
        compiler_params=_params("arbitrary", "arbitrary"),
        name=f"dil_decode{g}",
    )(buf_t, qg_t, kvn_t, sl)


def _dil_merge_kernel(*refs, alpha):
    states = refs[:3 * N_DIL_GROUPS]
    qt_ref, kvn_ref, w_ref, x_ref, g_ref, b_ref, out_ref = refs[3 * N_DIL_GROUPS:]
    dg = w_ref.shape[0]
    outs, lses = [], []
    for g in range(N_DIL_GROUPS):
        ot_ref, mt_ref, lt_ref = states[3 * g:3 * g + 3]
        qs =(qt_ref[g * dg:(g + 1) * dg, :] * ATTN_SCALE).astype(BF16).astype(F32)
        kn = kvn_ref[2 * g * dg:(2 * g + 1) * dg, :]
        vn = kvn_ref[(2 * g + 1) * dg:(2 * g + 2) * dg, :]
        s_new = _seg_sum(qs * kn, HEAD_DIM)
        m_p = mt_ref[...]
        m = jnp.maximum(m_p, s_new)
        a_p = jnp.exp(m_p - m)
        a_n = jnp.exp(s_new - m)
        l = lt_ref[...] * a_p + a_n
        o = (ot_ref[...] * _seg_expand(a_p, HEAD_DIM) + _seg_expand(a_n, HEAD_DIM) * vn) \
            / _seg_expand(l, HEAD_DIM)
        outs.append(o)
        lses.append(m + jnp.log(l))
    m = jnp.maximum(jnp.maximum(lses[0], lses[1]), lses[2])
    es = [jnp.exp(l - m) for l in lses]
    tot = es[0] + es[1] + es[2]
    o_t = sum(_seg_expand(e / tot, HEAD_DIM) * o for e, o in zip(es, outs))
    mix = _dot(o_t.T.astype(BF16), w_ref[...])
    out_ref[...] = _layer_norm(alpha * x_ref[...] + mix, g_ref[...], b_ref[...])


def _dil_merge(states, q_t, kvn_t, w, x, g, b, alpha):
    flat = [a for st in states for a in st]
    return pl.pallas_call(
        functools.partial(_dil_merge_kernel, alpha=alpha),
        out_shape=jax.ShapeDtypeStruct(x.shape, F32),
        compiler_params=pltpu.CompilerParams(vmem_limit_bytes=VMEM_LIMIT),
        name="dil_merge",
    )(*flat, q_t, kvn_t, w, x, g, b)


def kernel(x_prompt, x_sample, cache_fox_k, cache_fox_v, cache_fox_logf, cache_dil_kv_0, cache_dil_kv_1,
           cache_dil_kv_2, page_table, w_qkvf_a, b_f_a, w_o_a, w_q_b, w_kv_b, w_o_b, ln_g, ln_b,
           w_router_grp, b_router_grp, w_router_exp, b_router_exp, w_gate, w_up, w_down):
    nb, t, d = x_prompt.shape
    ns = x_sample.shape[0]
    depth = w_gate.shape[0]
    alpha = (2 * depth) ** 0.25
    nh = d // HEAD_DIM
    dg = w_o_b.shape[1]
    dq = N_DIL_GROUPS * dg

    w_a = w_qkvf_a[0]
    wk = w_a[:, d:2 * d].astype(BF16)
    wqkv_t = w_a[:, :3 * d].T.astype(BF16)
    wf = w_a[:, 3 * d:].astype(BF16)
    wf_t = wf.T
    b_f = b_f_a[0]
    wo_a = w_o_a[0].astype(BF16)
    wqkv_b_t = jnp.concatenate([w_q_b[0], w_kv_b], axis=1).T.astype(BF16)
    wkv_t = w_kv_b.T.astype(BF16)
    wo_b = w_o_b[0].astype(BF16)
    wqkv_g = [jnp.concatenate([w_q_b[0][:, g * dg:(g + 1) * dg] * ATTN_SCALE,
                               w_kv_b[:, 2 * g * dg:(2 * g + 2) * dg]], axis=1).astype(BF16)
              for g in range(N_DIL_GROUPS)]
    pad = LANES - N_EXPERT_GROUPS - N_EXPERTS
    wr = [jnp.pad(jnp.concatenate([w_router_grp[l], w_router_exp[l]], axis=1), ((0, 0), (0, pad)))
          for l in range(depth)]
    br = [jnp.pad(jnp.concatenate([b_router_grp[l], b_router_exp[l]]), (0, pad)).reshape(1, LANES)
          for l in range(depth)]
    wr_t = [jnp.concatenate([w_router_grp[l].T, w_router_exp[l].T, jnp.zeros((4, d), F32)], axis=0)
            for l in range(depth)]
    br_t = [jnp.concatenate([b_router_grp[l], b_router_exp[l], jnp.zeros((4,), F32)]).reshape(-1, 1)
            for l in range(depth)]
    lng = lambda l, i: ln_g[l, i].reshape(1, d)
    lnb = lambda l, i: ln_b[l, i].reshape(1, d)

    def moe(x, l):
        return _moe_norm(x, wr[l], br[l], w_gate[l], w_up[l], w_down[l], lng(l, 1), lnb(l, 1), alpha)

    def moe_routed(x, l):
        return _moe_routed_norm(x, wr_t[l], br_t[l], w_gate[l], w_up[l], w_down[l], lng(l, 1), lnb(l, 1), alpha)

    xp = x_prompt.reshape(nb * t, d)
    (q_t,) = _mm_t(wqkv_t[:d], x_prompt, (BF16,), scale=ATTN_SCALE * LOG2E)
    k_rm = _mm(xp, wk, BF16).reshape(nb, t, d)
    (k_t,) = _mm_t(wqkv_t[d:2 * d], x_prompt, (F32,))
    v_t, v_tb = _mm_t(wqkv_t[2 * d:], x_prompt, (F32, BF16))
    lf_t, cum_tok = _gate(x_prompt, wf_t, wf, b_f)
    o_t = _fox_attn(q_t, k_rm, v_tb, cum_tok)
    x1 = _proj_norm_t(o_t, wo_a, x_prompt, lng(0, 0), lnb(0, 0), alpha)
    x2 = moe_routed(x1.reshape(nb * t, d), 0).reshape(nb, t, d)

    tail = min(max(DIL_WINDOWS), t)
    (kv_tail_t,) = _mm_t(wkv_t, x2, (F32,), row_start=t - tail, rows=tail)
    os_, ls_ = zip(*[_dil_attn(_mm_dil(x2, wqkv_g[g], DIL_RATES[g]), g) for g in range(N_DIL_GROUPS)])
    x3 = _mix_norm(os_, ls_, wo_b, x2, lng(1, 0), lnb(1, 0), alpha)
    y_prompt = moe_routed(x3.reshape(nb * t, d), 1).reshape(nb, t, d)

    def feat_major_out(a_t, lead):
        b_, f_, t_ = a_t.shape
        return a_t.reshape(*lead, b_, f_ // HEAD_DIM, HEAD_DIM, t_).transpose(
            *range(len(lead)), len(lead), len(lead) + 3, len(lead) + 1, len(lead) + 2)

    fox_k_p = feat_major_out(k_t, (1,))
    fox_v_p = feat_major_out(v_t, (1,))
    fox_lf_p = lf_t.transpose(0, 2, 1)[None]
    dil_p = []
    for g, w in enumerate(DIL_WINDOWS):
        wlen = min(w, t)
        sl = kv_tail_t[:, 2 * g * dg:(2 * g + 2) * dg, tail - wlen:]
        dil_p.append(sl.reshape(nb, 2, dg // HEAD_DIM, HEAD_DIM, wlen).transpose(0, 4, 1, 2, 3))

    xs = x_sample.reshape(ns, d)
    (qkv_s,) = _mm_t(wqkv_t, xs[None], (F32,))
    q_st, kn_t, vn_t = qkv_s[0, :d], qkv_s[0, d:2 * d], qkv_s[0, 2 * d:]
    lfn_t, _ = _gate(xs[None], wf_t, wf, b_f)
    lfn_t = lfn_t[0]
    n_phys, ps = cache_fox_k.shape[1], cache_fox_k.shape[2]
    k_pages = cache_fox_k[0].transpose(0, 2, 3, 1).reshape(n_phys, d, ps)
    v_pages = cache_fox_v[0].transpose(0, 2, 3, 1).reshape(n_phys, d, ps)
    lf_pages = cache_fox_logf[0].transpose(0, 2, 1)
    o_t, m_t, l_t, c_t = _fox_decode(page_table, q_st, k_pages, v_pages, lf_pages)
    x1s = _fox_merge(o_t, m_t, l_t, c_t, q_st, kn_t, vn_t, lfn_t, wo_a, xs, lng(0, 0), lnb(0, 0), alpha)
    x2s = moe(x1s, 0)

    (qkv_b,) = _mm_t(wqkv_b_t, x2s[None], (F32,))
    qb_t, kvn_t = qkv_b[0, :dq], qkv_b[0, dq:]
    states, dil_s = [], []
    for g, cache in enumerate((cache_dil_kv_0, cache_dil_kv_1, cache_dil_kv_2)):
        wlen = cache.shape[1]
        buf_t = cache.transpose(0, 2, 3, 4, 1).reshape(ns, 2 * dg, wlen)
        new_buf, og, mg, lg = _dil_decode(buf_t, qb_t[g * dg:(g + 1) * dg], kvn_t[2 * g * dg:(2 * g + 2) * dg], g)
        states.append((og, mg, lg))
        dil_s.append(new_buf.reshape(ns, 2, dg // HEAD_DIM, HEAD_DIM, wlen).transpose(0, 4, 1, 2, 3))
    x3s = _dil_merge(states, qb_t, kvn_t, wo_b, x2s, lng(1, 0), lnb(1, 0), alpha)
    y_sample = moe(x3s, 1).reshape(ns, 1, d)

    fox_k_s = kn_t.reshape(1, 1, nh, HEAD_DIM, ns).transpose(0, 4, 1, 2, 3)
    fox_v_s = vn_t.reshape(1, 1, nh, HEAD_DIM, ns).transpose(0, 4, 1, 2, 3)
    fox_lf_s = lfn_t.reshape(1, 1, nh, ns).transpose(0, 3, 1, 2)

    return (y_prompt, y_sample, fox_k_p, fox_v_p, fox_lf_p, dil_p[0], dil_p[1], dil_p[2],
            fox_k_s, fox_v_s, fox_lf_s, dil_s[0], dil_s[1], dil_s[2])
```

```python
import functools

import jax
import jax.numpy as jnp
from jax import lax
from jax.experimental import pallas as pl
from jax.experimental.pallas import tpu as pltpu

F32 = jnp.float32
BF16 = jnp.bfloat16

HEAD_DIM = 64
LANES = 128
HEADS_PER_SLAB = LANES // HEAD_DIM
DIL_WINDOWS = (128, 512, 2048)
DIL_RATES = (1, 4, 16)
N_DIL_GROUPS = len(DIL_WINDOWS)
DIL_KEYS = 128
N_EXPERT_GROUPS = 4
EXPERTS_PER_GROUP = 8
N_EXPERTS = N_EXPERT_GROUPS * EXPERTS_PER_GROUP
LN_EPS = 1e-5
ATTN_SCALE = HEAD_DIM ** -0.5
LOG2E = 1.4426950408889634
BF16_ROWS = 16
ATTN_UNROLL = 2
NEG = -1e30
VMEM_LIMIT = 48 * 1024 * 1024
DECODE_BLOCK_BYTES = 4 * 1024 * 1024


def _params(*sem):
    return pltpu.CompilerParams(dimension_semantics=sem, vmem_limit_bytes=VMEM_LIMIT)


def _dot(a, b):
    return jnp.dot(a, b, preferred_element_type=F32)


def _dot_nt(a, b):
    return lax.dot_general(a, b, (((1,), (1,)), ((), ())), preferred_element_type=F32)


def _split3(x):
    hi = x.astype(BF16)
    r1 = x - hi.astype(F32)
    mid = r1.astype(BF16)
    lo = (r1 - mid.astype(F32)).astype(BF16)
    return hi, mid, lo


def _layer_norm(z, g, b):
    mu = jnp.mean(z, axis=-1, keepdims=True)
    zc = z - mu
    var = jnp.mean(zc * zc, axis=-1, keepdims=True)
    return zc * lax.rsqrt(var + LN_EPS) * g + b


def _seg_sum(x, seg):
    return jnp.sum(x.reshape(x.shape[0] // seg, seg, x.shape[1]), axis=1)


def _seg_expand(x, seg):
    h, l = x.shape
    return jnp.broadcast_to(x[:, None, :], (h, seg, l)).reshape(h * seg, l)


def _lane_col(x, idx):
    lane = lax.broadcasted_iota(jnp.int32, x.shape, 1)
    return jnp.sum(jnp.where(lane == idx, x, 0.0), axis=1, keepdims=True)


def _put_lane(ref_val, col, idx):
    lane = lax.broadcasted_iota(jnp.int32, ref_val.shape, 1)
    return jnp.where(lane == idx, col, ref_val)


def _upper_tri():
    r = jnp.arange(LANES)
    return (r[:, None] <= r[None, :]).astype(BF16)


def _mm_kernel(x_ref, w_ref, o_ref, *, scale):
    acc = _dot(x_ref[...].astype(BF16), w_ref[...])
    if scale is not None:
        acc = acc * scale
    o_ref[...] = acc.astype(o_ref.dtype)


def _mm(x, w, out_dtype, scale=None, tm=1024, tn=1024):
    n, k = x.shape
    m = w.shape[1]
    tm, tn = min(tm, n), min(tn, m)
    assert n % tm == 0 and m % tn == 0
    return pl.pallas_call(
        functools.partial(_mm_kernel, scale=scale),
        grid=(n // tm, m // tn),
        in_specs=[pl.BlockSpec((tm, k), lambda i, j: (i, 0)),
                  pl.BlockSpec((k, tn), lambda i, j: (0, j))],
        out_specs=pl.BlockSpec((tm, tn), lambda i, j: (i, j)),
        out_shape=jax.ShapeDtypeStruct((n, m), out_dtype),
        compiler_params=_params("parallel", "parallel"),
        name="mm",
    )(x, w)


def _mm_t_kernel(w_ref, x_ref, *o_refs, scale):
    acc = _dot_nt(w_ref[...], x_ref[0].astype(BF16))
    if scale is not None:
        acc = acc * scale
    for o_ref in o_refs:
        o_ref[0] = acc.astype(o_ref.dtype)


def _mm_t(w_t, x, out_dtypes, row_start=0, rows=None, scale=None, tm=1024, tn=1024):
    b, t, k = x.shape
    m = w_t.shape[0]
    rows = t - row_start if rows is None else rows
    tm, tn = min(tm, rows), min(tn, m)
    if m % tn:
        tn //= 2
    assert rows % tm == 0 and m % tn == 0 and row_start % tm == 0
    off = row_start // tm
    return pl.pallas_call(
        functools.partial(_mm_t_kernel, scale=scale),
        grid=(b, rows // tm, m // tn),
        in_specs=[pl.BlockSpec((tn, k), lambda bi, i, j: (j, 0)),
                  pl.BlockSpec((1, tm, k), lambda bi, i, j: (bi, i + off, 0))],
        out_specs=[pl.BlockSpec((1, tn, tm), lambda bi, i, j: (bi, j, i)) for _ in out_dtypes],
        out_shape=[jax.ShapeDtypeStruct((b, m, rows), dt) for dt in out_dtypes],
        compiler_params=_params("parallel", "parallel", "parallel"),
        name="mm_t",
    )(w_t, x)


def _log_sigmoid(y):
    return jnp.minimum(y, 0.0) - jnp.log1p(jnp.exp(-jnp.abs(y)))


def _gate_kernel(x_ref, wft_ref, wf_ref, bfc_ref, bfr_ref, tri_ref, lf_ref, cum_ref, carry_ref):
    @pl.when(pl.program_id(1) == 0)
    def _():
        carry_ref[...] = jnp.zeros_like(carry_ref)

    xb = x_ref[0].astype(BF16)
    lf_ref[0] = _log_sigmoid(_dot_nt(wft_ref[...], xb) + bfc_ref[...])
    lf = _log_sigmoid(_dot(xb, wf_ref[...]) + bfr_ref[...])
    hi, mid, lo = _split3(lf)
    tri = tri_ref[...]
    cs = _dot(tri, hi) + _dot(tri, mid) + _dot(tri, lo) + carry_ref[0:1, :]
    cum_ref[0] = cs * LOG2E
    carry_ref[...] = jnp.broadcast_to(cs[cs.shape[0] - 1:, :], carry_ref.shape)


def _gate(x, wf_t, wf, b_f, tm=512):
    b, t, k = x.shape
    h = wf_t.shape[0]
    tm = min(tm, t)
    r = jnp.arange(tm)
    tri = (r[None, :] <= r[:, None]).astype(BF16)
    fixed = lambda shape: pl.BlockSpec(shape, lambda bi, i: (0, 0))
    return pl.pallas_call(
        _gate_kernel,
        grid=(b, t // tm),
        in_specs=[pl.BlockSpec((1, tm, k), lambda bi, i: (bi, i, 0)),
                  fixed((h, k)), fixed((k, h)), fixed((h, 1)), fixed((1, h)), fixed((tm, tm))],
        out_specs=[pl.BlockSpec((1, h, tm), lambda bi, i: (bi, 0, i)),
                   pl.BlockSpec((1, tm, h), lambda bi, i: (bi, i, 0))],
        out_shape=[jax.ShapeDtypeStruct((b, h, t), F32), jax.ShapeDtypeStruct((b, t, h), F32)],
        scratch_shapes=[pltpu.VMEM((8, h), F32)],
        compiler_params=_params("arbitrary", "arbitrary"),
        name="gate",
    )(x, wf_t, wf, b_f.reshape(h, 1), b_f.reshape(1, h), tri)


def _fox_attn_kernel(qt_ref, k_ref, vt_ref, c_ref, o_ref,
                     s0_ref, s1_ref, p0_ref, p1_ref, al0_ref, al1_ref, acc_ref, m_ref, *, tq):
    iq = pl.program_id(2)
    qt = qt_ref[0]
    row = lax.broadcasted_iota(jnp.int32, qt.shape, 0)
    zero = jnp.zeros_like(qt)
    qtm = [jnp.where(row < HEAD_DIM, qt, zero), jnp.where(row >= HEAD_DIM, qt, zero)]
    ones = jnp.ones((BF16_ROWS, tq), BF16)
    heads = range(HEADS_PER_SLAB)

    s_refs, p_refs, al_refs = (s0_ref, s1_ref), (p0_ref, p1_ref), (al0_ref, al1_ref)

    def scores(j, slot):
        start = pl.multiple_of(j * tq, tq)
        kc = k_ref[0, pl.ds(start, tq), :]
        for a in heads:
            s_refs[slot][a] = _dot(kc, qtm[a]) - c_ref[0, 0, pl.ds(start, tq), a:a + 1]

    def softmax(slot, masked):
        for a in heads:
            st = s_refs[slot][a]
            if masked:
                krow = lax.broadcasted_iota(jnp.int32, st.shape, 0)
                qcol = lax.broadcasted_iota(jnp.int32, st.shape, 1)
                st = jnp.where(krow <= qcol, st, NEG)
            m_old = m_ref[a]
            m_new = jnp.maximum(m_old, jnp.max(st, axis=0, keepdims=True))
            m_ref[a] = m_new
            al_refs[slot][a] = jnp.exp2(m_old - m_new)
            p_refs[slot][a] = jnp.exp2(st - m_new[0:1, :]).astype(BF16)

    def pv(j, slot):
        start = pl.multiple_of(jnp.maximum(j, 0) * tq, tq)
        for a in heads:
            vta = jnp.concatenate([vt_ref[0, a * HEAD_DIM:(a + 1) * HEAD_DIM, pl.ds(start, tq)], ones], axis=0)
            acc_ref[a] = al_refs[slot][a, 0:1, :] * acc_ref[a] + _dot(vta, p_refs[slot][a])

    def stage(j, slot):
        scores(j + 1, 1 - slot)
        softmax(slot, False)
        pv(j - 1, 1 - slot)

    scores(0, 0)
    acc_ref[...] = jnp.zeros_like(acc_ref)
    m_ref[...] = jnp.full_like(m_ref, NEG)
    al1_ref[...] = jnp.ones_like(al1_ref)
    p1_ref[...] = jnp.zeros_like(p1_ref)

    def body(jj, _):
        for u in range(ATTN_UNROLL):
            stage(ATTN_UNROLL * jj + u, u % 2)
        return 0

    lax.fori_loop(0, iq // ATTN_UNROLL, body, 0)
    done = iq - iq % ATTN_UNROLL
    for u in range(ATTN_UNROLL - 1):
        @pl.when(iq % ATTN_UNROLL > u)
        def _():
            stage(done + u, u % 2)

    odd = iq % 2 == 1

    for slot in (0, 1):
        @pl.when(odd == (slot == 1))
        def _():
            softmax(slot, True)
            pv(iq - 1, 1 - slot)
            pv(iq, slot)

    outs = [acc_ref[a, :HEAD_DIM, :] / acc_ref[a, HEAD_DIM:HEAD_DIM + 1, :] for a in heads]
    o_ref[0] = jnp.concatenate(outs, axis=0).astype(o_ref.dtype)


def _fox_attn(q_t, k, v_t, cum_tok, tq=512):
    b, d, t = q_t.shape
    n_slab = d // LANES
    tq = min(tq, t)
    c4 = cum_tok.reshape(b, t, n_slab, HEADS_PER_SLAB).transpose(0, 2, 1, 3)
    return pl.pallas_call(
        functools.partial(_fox_attn_kernel, tq=tq),
        grid=(b, n_slab, t // tq),
        in_specs=[pl.BlockSpec((1, LANES, tq), lambda bi, p, i: (bi, p, i)),
                  pl.BlockSpec((1, t, LANES), lambda bi, p, i: (bi, 0, p)),
                  pl.BlockSpec((1, LANES, t), lambda bi, p, i: (bi, p, 0)),
                  pl.BlockSpec((1, 1, t, HEADS_PER_SLAB), lambda bi, p, i: (bi, p, 0, 0))],
        out_specs=pl.BlockSpec((1, LANES, tq), lambda bi, p, i: (bi, p, i)),
        out_shape=jax.ShapeDtypeStruct((b, d, t), BF16),
        scratch_shapes=([pltpu.VMEM((HEADS_PER_SLAB, tq, tq), F32)] * 2
                        + [pltpu.VMEM((HEADS_PER_SLAB, tq, tq), BF16)] * 2
                        + [pltpu.VMEM((HEADS_PER_SLAB, 8, tq), F32)] * 2
                        + [pltpu.VMEM((HEADS_PER_SLAB, HEAD_DIM + BF16_ROWS, tq), F32),
                           pltpu.VMEM((HEADS_PER_SLAB, 8, tq), F32)]),
        compiler_params=_params("parallel", "parallel", "arbitrary"),
        name="fox_attn",
    )(q_t, k, v_t, c4)


def _proj_norm_t_kernel(at_ref, w_ref, x_ref, g_ref, b_ref, o_ref, *, alpha):
    mix = lax.dot_general(at_ref[0], w_ref[...], (((0,), (0,)), ((), ())), preferred_element_type=F32)
    o_ref[0] = _layer_norm(alpha * x_ref[0] + mix, g_ref[...], b_ref[...])


def _proj_norm_t(a_t, w, x, g, b, alpha, tm=1024):
    nb, ka, t = a_t.shape
    d = w.shape[1]
    tm = min(tm, t)
    fixed = lambda shape: pl.BlockSpec(shape, lambda bi, i: (0, 0))
    return pl.pallas_call(
        functools.partial(_proj_norm_t_kernel, alpha=alpha),
        grid=(nb, t // tm),
        in_specs=[pl.BlockSpec((1, ka, tm), lambda bi, i: (bi, 0, i)),
                  fixed((ka, d)),
                  pl.BlockSpec((1, tm, d), lambda bi, i: (bi, i, 0)),
                  fixed((1, d)), fixed((1, d))],
        out_specs=pl.BlockSpec((1, tm, d), lambda bi, i: (bi, i, 0)),
        out_shape=jax.ShapeDtypeStruct((nb, t, d), F32),
        compiler_params=_params("parallel", "parallel"),
        name="proj_norm_t",
    )(a_t, w, x, g, b)


def _route(logits):
    lane = lax.broadcasted_iota(jnp.int32, logits.shape, 1).astype(F32)
    far = float(LANES)
    gl = jnp.where(lane < N_EXPERT_GROUPS, logits, NEG)
    gmax = jnp.max(gl, axis=1, keepdims=True)
    gsel = jnp.min(jnp.where(gl == gmax, lane, far), axis=1, keepdims=True)
    psel = 1.0 / jnp.sum(jnp.exp(gl - gmax), axis=1, keepdims=True)
    lo = N_EXPERT_GROUPS + gsel * EXPERTS_PER_GROUP
    el = jnp.where((lane >= lo) & (lane < lo + EXPERTS_PER_GROUP), logits, NEG)
    v1 = jnp.max(el, axis=1, keepdims=True)
    i1 = jnp.min(jnp.where(el == v1, lane, far), axis=1, keepdims=True)
    el2 = jnp.where(lane == i1, NEG, el)
    v2 = jnp.max(el2, axis=1, keepdims=True)
    i2 = jnp.min(jnp.where(el2 == v2, lane, far), axis=1, keepdims=True)
    e2 = jnp.exp(v2 - v1)
    den = 1.0 + e2
    return jnp.where(lane == i1, psel / den, jnp.where(lane == i2, psel * e2 / den, 0.0))


def _moe_kernel(x_ref, wr_ref, br_ref, wg_ref, wu_ref, wd_ref, g_ref, b_ref, o_ref,
                xb_ref, gates_ref, acc_ref, *, alpha):
    e = pl.program_id(1)

    @pl.when(e == 0)
    def _():
        x = x_ref[...]
        xh = x.astype(BF16)
        xl = (x - xh.astype(F32)).astype(BF16)
        wr = wr_ref[...]
        wh = wr.astype(BF16)
        wl = (wr - wh.astype(F32)).astype(BF16)
        xb_ref[...] = xh
        logits = _dot(xh, wh) + _dot(xl, wh) + _dot(xh, wl) + br_ref[...]
        gates_ref[...] = _route(logits)
        acc_ref[...] = jnp.zeros_like(acc_ref)

    xb = xb_ref[...]
    hg = _dot(xb, wg_ref[0, 0].astype(BF16))
    hu = _dot(xb, wu_ref[0, 0].astype(BF16))
    gcol = _lane_col(gates_ref[...], e + N_EXPERT_GROUPS)
    h = hg * (1.0 / (1.0 + jnp.exp(-hg))) * hu
    acc_ref[...] += _dot((h * gcol).astype(BF16), wd_ref[0, 0].astype(BF16))

    @pl.when(e == pl.num_programs(1) - 1)
    def _():
        o_ref[...] = _layer_norm(alpha * x_ref[...] + acc_ref[...], g_ref[...], b_ref[...])


def _moe_norm(x, wr, br, layer, wg, wu, wd, g, b, alpha, tm=1024):
    n, d = x.shape
    _, ne, _, f = wg.shape
    tm = min(tm, n)
    return pl.pallas_call(
        functools.partial(_moe_kernel, alpha=alpha),
        grid=(n // tm, ne),
        in_specs=[pl.BlockSpec((tm, d), lambda i, e: (i, 0)),
                  pl.BlockSpec((d, LANES), lambda i, e: (0, 0)),
                  pl.BlockSpec((1, LANES), lambda i, e: (0, 0)),
                  pl.BlockSpec((1, 1, d, f), lambda i, e: (layer, e, 0, 0)),
                  pl.BlockSpec((1, 1, d, f), lambda i, e: (layer, e, 0, 0)),
                  pl.BlockSpec((1, 1, f, d), lambda i, e: (layer, e, 0, 0)),
                  pl.BlockSpec((1, d), lambda i, e: (0, 0)),
                  pl.BlockSpec((1, d), lambda i, e: (0, 0))],
        out_specs=pl.BlockSpec((tm, d), lambda i, e: (i, 0)),
        out_shape=jax.ShapeDtypeStruct((n, d), F32),
        scratch_shapes=[pltpu.VMEM((tm, d), BF16), pltpu.VMEM((tm, LANES), F32), pltpu.VMEM((tm, d), F32)],
        compiler_params=_params("parallel", "arbitrary"),
        name="moe_norm",
    )(x, wr, br, wg, wu, wd, g, b)


def _route_t_kernel(x_ref, wr_ref, br_ref, u_ref, route_ref, cnt_ref, carry_ref):
    @pl.when(pl.program_id(0) == 0)
    def _():
        carry_ref[...] = jnp.zeros_like(carry_ref)

    x = x_ref[...]
    xh = x.astype(BF16)
    xl = (x - xh.astype(F32)).astype(BF16)
    w = wr_ref[...]
    wh = w.astype(BF16)
    wl = (w - wh.astype(F32)).astype(BF16)
    logits = _dot_nt(wh, xh) + _dot_nt(wh, xl) + _dot_nt(wl, xh) + br_ref[...]
    r = lax.broadcasted_iota(jnp.int32, logits.shape, 0).astype(F32)
    far = float(logits.shape[0])
    gl = jnp.where(r < N_EXPERT_GROUPS, logits, NEG)
    gmax = jnp.max(gl, axis=0, keepdims=True)
    gsel = jnp.min(jnp.where(gl == gmax, r, far), axis=0, keepdims=True)
    psel = 1.0 / jnp.sum(jnp.exp(gl - gmax), axis=0, keepdims=True)
    lo = N_EXPERT_GROUPS + gsel * EXPERTS_PER_GROUP
    el = jnp.where((r >= lo) & (r < lo + EXPERTS_PER_GROUP), logits, NEG)
    v1 = jnp.max(el, axis=0, keepdims=True)
    i1 = jnp.min(jnp.where(el == v1, r, far), axis=0, keepdims=True)
    el2 = jnp.where(r == i1, NEG, el)
    v2 = jnp.max(el2, axis=0, keepdims=True)
    i2 = jnp.min(jnp.where(el2 == v2, r, far), axis=0, keepdims=True)
    e2 = jnp.exp(v2 - v1)
    den = 1.0 + e2
    onehot = jnp.where(r == i1, 1.0, jnp.where(r == i2, 1.0, 0.0)).astype(BF16)
    cum = _dot(onehot, u_ref[...]) + carry_ref[:, 0:1]
    rank1 = jnp.sum(jnp.where(r == i1, cum, 0.0), axis=0, keepdims=True) - 1.0
    rank2 = jnp.sum(jnp.where(r == i2, cum, 0.0), axis=0, keepdims=True) - 1.0
    carry_ref[...] = jnp.broadcast_to(cum[:, cum.shape[1] - 1:], carry_ref.shape)
    cnt_ref[...] = carry_ref[...]
    rows = (i1 - N_EXPERT_GROUPS, i2 - N_EXPERT_GROUPS, psel / den, psel * e2 / den, rank1, rank2)
    r8 = lax.broadcasted_iota(jnp.int32, route_ref.shape, 0)
    out = jnp.zeros(route_ref.shape, F32)
    for k, v in enumerate(rows):
        out = jnp.where(r8 == k, v, out)
    route_ref[...] = out


def _route_t(x, wr_t, br_t, tm=512):
    n, d = x.shape
    nr = wr_t.shape[0]
    tm = min(tm, n)
    r = jnp.arange(tm)
    upper = (r[:, None] <= r[None, :]).astype(BF16)
    fixed = lambda shape: pl.BlockSpec(shape, lambda i: (0, 0))
    return pl.pallas_call(
        _route_t_kernel,
        grid=(n // tm,),
        in_specs=[pl.BlockSpec((tm, d), lambda i: (i, 0)), fixed((nr, d)), fixed((nr, 1)), fixed((tm, tm))],
        out_specs=[pl.BlockSpec((8, tm), lambda i: (0, i)), fixed((nr, LANES))],
        out_shape=[jax.ShapeDtypeStruct((8, n), F32), jax.ShapeDtypeStruct((nr, LANES), F32)],
        scratch_shapes=[pltpu.VMEM((nr, LANES), F32)],
        compiler_params=_params("arbitrary"),
        name="route",
    )(x, wr_t, br_t, upper)


def _positions_kernel(route_ref, off_ref, pos_ref):
    rt = route_ref[...]
    off = off_ref[...]
    r = lax.broadcasted_iota(jnp.int32, (off.shape[0], rt.shape[1]), 0).astype(F32) - float(N_EXPERT_GROUPS)
    rows = [jnp.sum(jnp.where(r == rt[k:k + 1], off, 0.0), axis=0, keepdims=True) + rt[4 + k:5 + k]
            for k in range(2)]
    pos_ref[0] = jnp.concatenate(rows, axis=0).astype(jnp.int32)


def _positions(route, offsets, tt):
    n = route.shape[1]
    return pl.pallas_call(
        _positions_kernel,
        grid=(n // tt,),
        in_specs=[pl.BlockSpec((8, tt), lambda i: (0, i)), pl.BlockSpec(offsets.shape, lambda i: (0, 0))],
        out_specs=pl.BlockSpec((1, 2, tt), lambda i: (i, 0, 0)),
        out_shape=jax.ShapeDtypeStruct((n // tt, 2, tt), jnp.int32),
        compiler_params=_params("parallel"),
        name="positions",
    )(route, offsets)


def _row_copies(idx_ref, tt, src_of, dst_of, sem):
    def each(fn):
        def body(j, carry):
            for k in range(2):
                p = idx_ref[k * tt + j]
                fn(pltpu.make_async_copy(src_of(k, j, p), dst_of(k, j, p), sem))
            return carry
        lax.fori_loop(0, tt, body, 0, unroll=8)
    each(lambda cp: cp.start())
    return lambda: each(lambda cp: cp.wait())


def _dispatch_kernel(tail_ref, used_ref, pos_ref, x_ref, xs_ref, idx_ref, zero_ref, sem_i, sem_r, sem_z,
                     *, tt, rows):
    i = pl.program_id(0)

    @pl.when(i == 0)
    def _():
        zero_ref[...] = jnp.zeros_like(zero_ref)
        copies = [pltpu.make_async_copy(zero_ref, xs_ref.at[pl.ds(tail_ref[e] * rows, rows), :], sem_z)
                  for e in range(tail_ref.shape[0])]
        for e, cp in enumerate(copies):
            pl.when(used_ref[e] > 0)(cp.start)
        for e, cp in enumerate(copies):
            pl.when(used_ref[e] > 0)(cp.wait)

    cp = pltpu.make_async_copy(pos_ref.at[pl.ds(i * 2 * tt, 2 * tt)], idx_ref, sem_i)
    cp.start()
    cp.wait()
    wait = _row_copies(idx_ref, tt,
                       lambda k, j, p: x_ref.at[pl.ds(j, 1), :],
                       lambda k, j, p: xs_ref.at[pl.ds(p, 1), :], sem_r)
    wait()


def _dispatch(tail_tile, used, pos_flat, x, n_rows, tt, rows):
    n, d = x.shape
    grid_spec = pltpu.PrefetchScalarGridSpec(
        num_scalar_prefetch=2,
        grid=(n // tt,),
        in_specs=[pl.BlockSpec(memory_space=pl.ANY), pl.BlockSpec((tt, d), lambda i, tail, used: (i, 0))],
        out_specs=pl.BlockSpec(memory_space=pl.ANY),
        scratch_shapes=[pltpu.SMEM((2 * tt,), jnp.int32), pltpu.VMEM((rows, d), F32)]
        + [pltpu.SemaphoreType.DMA(())] * 3,
    )
    return pl.pallas_call(
        functools.partial(_dispatch_kernel, tt=tt, rows=rows),
        grid_spec=grid_spec,
        out_shape=jax.ShapeDtypeStruct((n_rows, d), F32),
        compiler_params=_params("arbitrary"),
        name="dispatch",
    )(tail_tile, used, pos_flat, x)


def _expert_kernel(te_ref, blk_ref, nv_ref, xs_ref, wg_ref, wu_ref, wd_ref, ys_ref, wgb, wub, wdb):
    i = pl.program_id(0)

    @pl.when((i == 0) | (te_ref[i] != te_ref[jnp.maximum(i - 1, 0)]))
    def _():
        wgb[...] = wg_ref[0, 0].astype(BF16)
        wub[...] = wu_ref[0, 0].astype(BF16)
        wdb[...] = wd_ref[0, 0].astype(BF16)

    @pl.when(i < nv_ref[0])
    def _():
        xb = xs_ref[...].astype(BF16)
        hg = _dot(xb, wgb[...])
        hu = _dot(xb, wub[...])
        h = hg * (1.0 / (1.0 + jnp.exp(-hg))) * hu
        ys_ref[...] = _dot(h.astype(BF16), wdb[...])

    @pl.when(i >= nv_ref[0])
    def _():
        ys_ref[...] = jnp.zeros_like(ys_ref)


def _experts(tile_expert, tile_block, n_valid, xs, layer, wg, wu, wd, rows):
    n_rows, d = xs.shape
    f = wg.shape[3]
    n_tiles = tile_expert.shape[0]
    row_spec = pl.BlockSpec((rows, d), lambda i, te, blk, nv: (blk[i], 0))
    grid_spec = pltpu.PrefetchScalarGridSpec(
        num_scalar_prefetch=3,
        grid=(n_tiles,),
        in_specs=[row_spec,
                  pl.BlockSpec((1, 1, d, f), lambda i, te, blk, nv: (layer, te[i], 0, 0)),
                  pl.BlockSpec((1, 1, d, f), lambda i, te, blk, nv: (layer, te[i], 0, 0)),
                  pl.BlockSpec((1, 1, f, d), lambda i, te, blk, nv: (layer, te[i], 0, 0))],
        out_specs=row_spec,
        scratch_shapes=[pltpu.VMEM((d, f), BF16), pltpu.VMEM((d, f), BF16), pltpu.VMEM((f, d), BF16)],
    )
    return pl.pallas_call(
        _expert_kernel,
        grid_spec=grid_spec,
        out_shape=jax.ShapeDtypeStruct((n_rows, d), F32),
        compiler_params=_params("arbitrary"),
        name="experts",
    )(tile_expert, tile_block, n_valid, xs, wg, wu, wd)


def _combine_kernel(pos_ref, ys_ref, x_ref, gs_ref, sel_ref, g_ref, b_ref, o_ref, idx_ref, ybuf, sem_i, sem_r,
                    *, tt, alpha):
    i = pl.program_id(0)
    cp = pltpu.make_async_copy(pos_ref.at[pl.ds(i * 2 * tt, 2 * tt)], idx_ref, sem_i)
    cp.start()
    cp.wait()
    wait = _row_copies(idx_ref, tt,
                       lambda k, j, p: ys_ref.at[pl.ds(p, 1), :],
                       lambda k, j, p: ybuf.at[k, pl.ds(j, 1), :], sem_r)
    wcols = lax.dot_general(gs_ref[...], sel_ref[...], (((0,), (0,)), ((), ())), preferred_element_type=F32)
    w1, w2 = wcols[:, 0:1], wcols[:, LANES:LANES + 1]
    wait()
    z = alpha * x_ref[...] + w1 * ybuf[0] + w2 * ybuf[1]
    o_ref[...] = _layer_norm(z, g_ref[...], b_ref[...])


def _combine(pos_flat, ys, x, gate_split, g, b, alpha, tt):
    n, d = x.shape
    sel = jnp.zeros((8, 2 * LANES), BF16).at[0:2, :LANES].set(1).at[2:4, LANES:].set(1)
    fixed = lambda shape: pl.BlockSpec(shape, lambda i: (0, 0))
    return pl.pallas_call(
        functools.partial(_combine_kernel, tt=tt, alpha=alpha),
        grid=(n // tt,),
        in_specs=[pl.BlockSpec(memory_space=pl.ANY), pl.BlockSpec(memory_space=pl.ANY),
                  pl.BlockSpec((tt, d), lambda i: (i, 0)),
                  pl.BlockSpec((8, tt), lambda i: (0, i)),
                  fixed((8, 2 * LANES)), fixed((1, d)), fixed((1, d))],
        out_specs=pl.BlockSpec((tt, d), lambda i: (i, 0)),
        out_shape=jax.ShapeDtypeStruct((n, d), F32),
        scratch_shapes=[pltpu.SMEM((2 * tt,), jnp.int32), pltpu.VMEM((2, tt, d), F32),
                        pltpu.SemaphoreType.DMA(()), pltpu.SemaphoreType.DMA(())],
        compiler_params=_params("arbitrary"),
        name="combine",
    )(pos_flat, ys, x, gate_split, sel, g, b)


def _moe_routed_norm(x, wr_t, br_t, layer, wg, wu, wd, g, b, alpha, rows=256, tt=512):
    n, d = x.shape
    ne = wg.shape[1]
    route, counts = _route_t(x, wr_t, br_t)
    cnt = counts[N_EXPERT_GROUPS:N_EXPERT_GROUPS + ne, 0].astype(jnp.int32)
    tiles = (cnt + rows - 1) // rows
    tile_end = jnp.cumsum(tiles)
    offsets = jnp.zeros((counts.shape[0], 1), F32).at[N_EXPERT_GROUPS:N_EXPERT_GROUPS + ne, 0].set(
        ((tile_end - tiles) * rows).astype(F32))
    pos_flat = _positions(route, offsets, tt).reshape(-1)
    max_tiles = 2 * n // rows + ne
    n_valid = tile_end[ne - 1]
    tile_block = jnp.arange(max_tiles, dtype=jnp.int32)
    last = jnp.maximum(n_valid - 1, 0)
    tile_expert = jnp.minimum(jnp.sum(tile_end[None, :] <= jnp.minimum(tile_block, last)[:, None], axis=1),
                              ne - 1).astype(jnp.int32)
    gate = route[2:4]
    g_hi = gate.astype(BF16)
    g_lo = (gate - g_hi.astype(F32)).astype(BF16)
    gate_split = jnp.concatenate([g_hi[0:1], g_lo[0:1], g_hi[1:2], g_lo[1:2], jnp.zeros((4, n), BF16)], axis=0)

    trailing = n_valid + jnp.arange(ne, dtype=jnp.int32)
    zero_tile = jnp.concatenate([jnp.maximum(tile_end - 1, 0), jnp.minimum(trailing, max_tiles - 1)])
    zero_used = jnp.concatenate([tiles > 0, trailing < max_tiles]).astype(jnp.int32)

    xs = _dispatch(zero_tile.astype(jnp.int32), zero_used, pos_flat, x, max_tiles * rows, tt, rows)
    ys = _experts(tile_expert, tile_block, n_valid.reshape(1).astype(jnp.int32), xs, layer, wg, wu, wd, rows)
    return _combine(pos_flat, ys, x, gate_split, g, b, alpha, tt)


def _alibi_slope(g, h, n_heads):
    return 2.0 ** (-8.0 * (h * N_DIL_GROUPS + g + 1) / (N_DIL_GROUPS * n_heads))


def _mm_dil_kernel(x_ref, w_ref, o_ref, acc_ref, *, rate):
    acc = _dot(x_ref[0].astype(BF16), w_ref[...])
    if rate == 1:
        o_ref[0, 0] = acc.astype(o_ref.dtype)
        return
    rows = acc.shape[0] // rate
    for c in range(acc_ref.shape[0]):
        acc_ref[c] = acc[:, c * LANES:(c + 1) * LANES]
    for r in range(rate):
        for c in range(acc_ref.shape[0]):
            o_ref[0, r, :, c * LANES:(c + 1) * LANES] = \
                acc_ref[c, pl.ds(r, rows, stride=rate), :].astype(o_ref.dtype)


def _mm_dil(x, w, rate, tm=1024, tn=1536):
    b, t, k = x.shape
    m = w.shape[1]
    tm, tn = min(tm, t), min(tn, m)
    assert t % tm == 0 and m % tn == 0 and tm % rate == 0
    return pl.pallas_call(
        functools.partial(_mm_dil_kernel, rate=rate),
        grid=(b, t // tm, m // tn),
        in_specs=[pl.BlockSpec((1, tm, k), lambda bi, i, j: (bi, i, 0)),
                  pl.BlockSpec((k, tn), lambda bi, i, j: (0, j))],
        out_specs=pl.BlockSpec((1, rate, tm // rate, tn), lambda bi, i, j: (bi, 0, i, j)),
        out_shape=jax.ShapeDtypeStruct((b, rate, t // rate, m), BF16),
        scratch_shapes=[pltpu.VMEM((tn // LANES, tm, LANES), F32)],
        compiler_params=_params("parallel", "parallel", "parallel"),
        name=f"mm_dil{rate}",
    )(x, w)


def _dil_attn_kernel(q_ref, kc_ref, kp_ref, vc_ref, vp_ref, o_ref, lse_ref, kbuf, vbuf, *, tu, rate, slopes):
    i = pl.program_id(2)
    kbuf[0:DIL_KEYS] = kp_ref[0, 0]
    kbuf[DIL_KEYS:] = kc_ref[0, 0]
    vbuf[0:DIL_KEYS] = vp_ref[0, 0]
    vbuf[DIL_KEYS:] = vc_ref[0, 0]
    shape = (DIL_KEYS, 2 * DIL_KEYS)
    a = lax.broadcasted_iota(jnp.int32, shape, 0)
    c = lax.broadcasted_iota(jnp.int32, shape, 1)
    dist = a - c + DIL_KEYS
    base = jnp.where((dist >= 0) & (dist <= DIL_KEYS), (-rate * dist).astype(F32), NEG)
    base_first = jnp.where(c >= DIL_KEYS, base, NEG)
    lane = lax.broadcasted_iota(jnp.int32, (DIL_KEYS, LANES), 1)
    left = lane < HEAD_DIM
    for s in range(tu // DIL_KEYS):
        bias = jnp.where(i == 0, base_first, base) if s == 0 else base
        r0 = s * DIL_KEYS
        for p in range(q_ref.shape[3] // LANES):
            cs = slice(p * LANES, (p + 1) * LANES)
            qp = q_ref[0, 0, r0:r0 + DIL_KEYS, cs]
            kk = kbuf[r0:r0 + 2 * DIL_KEYS, cs]
            vv = vbuf[r0:r0 + 2 * DIL_KEYS, cs]
            outs, lses = [], []
            for hh in range(HEADS_PER_SLAB):
                qm = jnp.where(left if hh == 0 else jnp.logical_not(left), qp, jnp.zeros_like(qp))
                sc = _dot_nt(qm, kk) + slopes[HEADS_PER_SLAB * p + hh] * bias
                m = jnp.max(sc, axis=1, keepdims=True)
                e = jnp.exp(sc - m)
                l = jnp.sum(e, axis=1, keepdims=True)
                outs.append(_dot(e.astype(BF16), vv) / l)
                lses.append(m + jnp.log(l))
            o_ref[0, 0, r0:r0 + DIL_KEYS, cs] = jnp.where(left, outs[0], outs[1])
            lse_ref[0, 0, r0:r0 + DIL_KEYS, cs] = jnp.where(left, lses[0], lses[1])


def _dil_attn(qkv, g, tu=512):
    b, rate, u, dg3 = qkv.shape
    dg = dg3 // 3
    tu = min(tu, u)
    ratio = tu // DIL_KEYS
    slopes = tuple(_alibi_slope(g, h, dg // HEAD_DIM) for h in range(dg // HEAD_DIM))
    cur = lambda col: pl.BlockSpec((1, 1, tu, dg), lambda bi, r, i: (bi, r, i, col))
    prev = lambda col: pl.BlockSpec((1, 1, DIL_KEYS, dg),
                                    lambda bi, r, i: (bi, r, jnp.maximum(i * ratio - 1, 0), col))
    return pl.pallas_call(
        functools.partial(_dil_attn_kernel, tu=tu, rate=rate, slopes=slopes),
        grid=(b, rate, u // tu),
        in_specs=[cur(0), cur(1), prev(1), cur(2), prev(2)],
        out_specs=[cur(0)] * 2,
        out_shape=[jax.ShapeDtypeStruct((b, rate, u, dg), F32)] * 2,
        scratch_shapes=[pltpu.VMEM((tu + DIL_KEYS, dg), BF16)] * 2,
        compiler_params=_params("parallel", "parallel", "arbitrary"),
        name=f"dil_attn{g}",
    )(qkv, qkv, qkv, qkv, qkv)


def _mix_norm_kernel(*refs, alpha, rates):
    ng = len(rates)
    o_refs, l_refs = refs[:ng], refs[ng:2 * ng]
    w_ref, x_ref, g_ref, b_ref, out_ref = refs[2 * ng:2 * ng + 5]
    bufs = refs[2 * ng + 5:]

    def natural(ref, buf, rate):
        if rate == 1:
            return ref[0, 0]
        for r in range(rate):
            for c in range(buf.shape[0]):
                buf[c, pl.ds(r, ref.shape[2], stride=rate), :] = ref[0, r, :, c * LANES:(c + 1) * LANES]
        return jnp.concatenate([buf[c] for c in range(buf.shape[0])], axis=1)

    os_ = [natural(o_refs[g], bufs[g], rates[g]) for g in range(ng)]
    ls = [natural(l_refs[g], bufs[ng + g], rates[g]) for g in range(ng)]
    m = functools.reduce(jnp.maximum, ls)
    es = [jnp.exp(l - m) for l in ls]
    o = sum(e * o_ for e, o_ in zip(es, os_)) / sum(es)
    mix = _dot(o.astype(BF16), w_ref[...])
    out_ref[0] = _layer_norm(alpha * x_ref[0] + mix, g_ref[...], b_ref[...])


def _mix_norm(os_, ls, w, x, g, b, alpha, tm=512):
    nb, t, d = x.shape
    dg = w.shape[0]
    tm = min(tm, t)
    rates = tuple(o.shape[1] for o in os_)
    grp = lambda rate: pl.BlockSpec((1, rate, tm // rate, dg), lambda bi, i: (bi, 0, i, 0))
    fixed = lambda shape: pl.BlockSpec(shape, lambda bi, i: (0, 0))
    xrow = pl.BlockSpec((1, tm, d), lambda bi, i: (bi, i, 0))
    return pl.pallas_call(
        functools.partial(_mix_norm_kernel, alpha=alpha, rates=rates),
        grid=(nb, t // tm),
        in_specs=[grp(r) for r in rates] * 2 + [fixed((dg, d)), xrow, fixed((1, d)), fixed((1, d))],
        out_specs=xrow,
        out_shape=jax.ShapeDtypeStruct((nb, t, d), F32),
        scratch_shapes=[pltpu.VMEM((dg // LANES, tm, LANES), F32)] * (2 * len(rates)),
        compiler_params=_params("parallel", "parallel"),
        name="mix_norm",
    )(*os_, *ls, w, x, g, b)


def _fox_decode_kernel(pt_ref, qt_ref, u_ref, *refs, pages):
    k_refs, v_refs, lf_refs = refs[:pages], refs[pages:2 * pages], refs[2 * pages:3 * pages]
    ot_ref, mt_ref, lt_ref, ct_ref, qcol_ref, acc_ref, m_ref, l_ref, c_ref = refs[3 * pages:]
    b, j = pl.program_id(0), pl.program_id(1)
    nh = m_ref.shape[0]

    @pl.when((b == 0) & (j == 0))
    def _():
        for r in (ot_ref, mt_ref, lt_ref, ct_ref):
            r[...] = jnp.zeros_like(r)

    @pl.when(j == 0)
    def _():
        qcol = _lane_col(qt_ref[...], b) * ATTN_SCALE
        qcol_ref[...] = jnp.broadcast_to(qcol.astype(BF16).astype(F32), qcol_ref.shape)
        acc_ref[...] = jnp.zeros_like(acc_ref)
        m_ref[...] = jnp.full_like(m_ref, NEG)
        l_ref[...] = jnp.zeros_like(l_ref)
        c_ref[...] = jnp.zeros_like(c_ref)

    u = u_ref[...]
    for pi in range(pages):
        hi, mid, lo = _split3(lf_refs[pi][0])
        cs = _dot(hi, u) + _dot(mid, u) + _dot(lo, u) + c_ref[...]
        c_ref[...] = jnp.broadcast_to(cs[:, LANES - 1:LANES], c_ref.shape)
        s = _seg_sum(qcol_ref[...] * k_refs[pi][0], HEAD_DIM) - cs
        m_old = m_ref[...]
        m_new = jnp.maximum(m_old, jnp.max(s, axis=1, keepdims=True))
        alpha = jnp.exp(m_old - m_new)
        p = jnp.exp(s - m_new)
        m_ref[...] = m_new
        l_ref[...] = alpha * l_ref[...] + p
        acc_ref[...] = (acc_ref[...] * _seg_expand(alpha, HEAD_DIM)
                        + _seg_expand(p, HEAD_DIM) * v_refs[pi][0])

    @pl.when(j == pl.num_programs(1) - 1)
    def _():
        ot_ref[...] = _put_lane(ot_ref[...], jnp.sum(acc_ref[...], axis=1, keepdims=True), b)
        lt_ref[...] = _put_lane(lt_ref[...], jnp.sum(l_ref[...], axis=1, keepdims=True), b)
        mt_ref[...] = _put_lane(mt_ref[...], m_ref[:, 0:1], b)
        ct_ref[...] = _put_lane(ct_ref[...], c_ref[:, 0:1], b)


def _fox_decode(page_table, q_t, k_pages, v_pages, lf_pages, pages=8):
    d, nb = q_t.shape
    nh = lf_pages.shape[1]
    n_pages = page_table.shape[1]
    ps = k_pages.shape[2]
    pt = page_table.reshape(-1)

    def page_spec(rows, pi):
        return pl.BlockSpec((1, rows, ps), lambda b, j, pt_ref: (pt_ref[b * n_pages + j * pages + pi], 0, 0))

    fixed = lambda shape: pl.BlockSpec(shape, lambda b, j, pt_ref: (0, 0))
    grid_spec = pltpu.PrefetchScalarGridSpec(
        num_scalar_prefetch=1,
        grid=(nb, n_pages // pages),
        in_specs=([fixed((d, nb)), fixed((LANES, LANES))]
                  + [page_spec(d, pi) for pi in range(pages)] * 2
                  + [page_spec(nh, pi) for pi in range(pages)]),
        out_specs=[fixed((d, nb)), fixed((nh, nb)), fixed((nh, nb)), fixed((nh, nb))],
        scratch_shapes=[pltpu.VMEM((d, ps), F32), pltpu.VMEM((d, ps), F32),
                        pltpu.VMEM((nh, ps), F32), pltpu.VMEM((nh, ps), F32), pltpu.VMEM((nh, ps), F32)],
    )
    return pl.pallas_call(
        functools.partial(_fox_decode_kernel, pages=pages),
        grid_spec=grid_spec,
        out_shape=[jax.ShapeDtypeStruct((d, nb), F32)] + [jax.ShapeDtypeStruct((nh, nb), F32)] * 3,
        compiler_params=_params("arbitrary", "arbitrary"),
        name="fox_decode",
    )(pt, q_t, _upper_tri(), *([k_pages] * pages), *([v_pages] * pages), *([lf_pages] * pages))


def _fox_merge_kernel(ot_ref, mt_ref, lt_ref, ct_ref, qt_ref, kn_ref, vn_ref, lfn_ref,
                      w_ref, x_ref, g_ref, b_ref, out_ref, *, alpha):
    qs = (qt_ref[...] * ATTN_SCALE).astype(BF16).astype(F32)
    s_new = _seg_sum(qs * kn_ref[...], HEAD_DIM) - (ct_ref[...] + lfn_ref[...])
    m_p = mt_ref[...]
    m = jnp.maximum(m_p, s_new)
    a_p = jnp.exp(m_p - m)
    a_n = jnp.exp(s_new - m)
    l = lt_ref[...] * a_p + a_n
    o_t = (ot_ref[...] * _seg_expand(a_p, HEAD_DIM) + _seg_expand(a_n, HEAD_DIM) * vn_ref[...]) \
        / _seg_expand(l, HEAD_DIM)
    mix = _dot(o_t.T.astype(BF16), w_ref[...])
    out_ref[...] = _layer_norm(alpha * x_ref[...] + mix, g_ref[...], b_ref[...])


def _fox_merge(o_t, m_t, l_t, c_t, q_t, kn_t, vn_t, lfn_t, w, x, g, b, alpha):
    return pl.pallas_call(
        functools.partial(_fox_merge_kernel, alpha=alpha),
        out_shape=jax.ShapeDtypeStruct(x.shape, F32),
        compiler_params=pltpu.CompilerParams(vmem_limit_bytes=VMEM_LIMIT),
        name="fox_merge",
    )(o_t, m_t, l_t, c_t, q_t, kn_t, vn_t, lfn_t, w, x, g, b)


def _dil_decode_kernel(buf_ref, qt_ref, kvn_ref, sl_ref, out_ref, ot_ref, mt_ref, lt_ref, p_ref,
                       *, window, rate):
    bi, c = pl.program_id(0), pl.program_id(1)

    @pl.when((bi == 0) & (c == 0))
    def _():
        for r in (ot_ref, mt_ref, lt_ref):
            r[...] = jnp.zeros_like(r)

    t = lax.broadcasted_iota(jnp.int32, (1, window), 1)
    for bb in range(buf_ref.shape[0]):
        b = bi * buf_ref.shape[0] + bb
        x = buf_ref[bb]

        @pl.when(c == 0)
        def _():
            qcol = (_lane_col(qt_ref[...], b) * ATTN_SCALE).astype(BF16).astype(F32)
            s = _seg_sum(x * qcol, HEAD_DIM)
            dist = (window - t).astype(F32)
            s = jnp.where((t & (rate - 1)) == 0, s - sl_ref[:, 0:1] * dist, NEG)
            m = jnp.max(s, axis=1, keepdims=True)
            p = jnp.exp(s - m)
            p_ref[bb] = p
            mt_ref[...] = _put_lane(mt_ref[...], m, b)
            lt_ref[...] = _put_lane(lt_ref[...], jnp.sum(p, axis=1, keepdims=True), b)

        @pl.when(c == 1)
        def _():
            o = jnp.sum(x * _seg_expand(p_ref[bb], HEAD_DIM), axis=1, keepdims=True)
            ot_ref[...] = _put_lane(ot_ref[...], o, b)

        newcol = _lane_col(kvn_ref[...], b)
        shifted = pltpu.roll(x, window - 1, axis=1)
        out_ref[bb] = jnp.where(t == window - 1, newcol, shifted)


def _dil_decode(buf_t, qg_t, kvn_t, g):
    nb, _, window = buf_t.shape
    dg = qg_t.shape[0]
    nh = dg // HEAD_DIM
    rate = DIL_RATES[g]
    assert rate & (rate - 1) == 0 and window == DIL_KEYS * rate
    slopes = jnp.asarray([_alibi_slope(g, h, nh) for h in range(nh)], F32)
    sl = jnp.broadcast_to(slopes.reshape(nh, 1), (nh, LANES))
    fixed = lambda shape: pl.BlockSpec(shape, lambda b, c: (0, 0))
    per_step = max(1, min(nb, DECODE_BLOCK_BYTES // (dg * window * 4)))
    assert nb % per_step == 0
    return pl.pallas_call(
        functools.partial(_dil_decode_kernel, window=window, rate=rate),
        grid=(nb // per_step, 2),
        in_specs=[pl.BlockSpec((per_step, dg, window), lambda b, c: (b, c, 0)),
                  fixed((dg, nb)),
                  pl.BlockSpec((dg, nb), lambda b, c: (c, 0)),
                  fixed((nh, LANES))],
        out_specs=[pl.BlockSpec((per_step, dg, window), lambda b, c: (b, c, 0)),
                   fixed((dg, nb)), fixed((nh, nb)), fixed((nh, nb))],
        out_shape=[jax.ShapeDtypeStruct(buf_t.shape, F32),
                   jax.ShapeDtypeStruct((dg, nb), F32),
                   jax.ShapeDtypeStruct((nh, nb), F32),
                   jax.ShapeDtypeStruct((nh, nb), F32)],
        scratch_shapes=[pltpu.VMEM((per_step, nh, window), F32)],
        compiler_params=_params("arbitrary", "arbitrary"),
        name=f"dil_decode{g}",
    )(buf_t, qg_t, kvn_t, sl)


def _dil_merge_kernel(*refs, alpha):
    states = refs[:3 * N_DIL_GROUPS]
    qt_ref, kvn_ref, w_ref, x_ref, g_ref, b_ref, out_ref = refs[3 * N_DIL_GROUPS:]
    dg = w_ref.shape[0]
    outs, lses = [], []
    for g in range(N_DIL_GROUPS):
        ot_ref, mt_ref, lt_ref = states[3 * g:3 * g + 3]
        qs =(qt_ref[g * dg:(g + 1) * dg, :] * ATTN_SCALE).astype(BF16).astype(F32)
        kn = kvn_ref[2 * g * dg:(2 * g + 1) * dg, :]
        vn = kvn_ref[(2 * g + 1) * dg:(2 * g + 2) * dg, :]
        s_new = _seg_sum(qs * kn, HEAD_DIM)
        m_p = mt_ref[...]
        m = jnp.maximum(m_p, s_new)
        a_p = jnp.exp(m_p - m)
        a_n = jnp.exp(s_new - m)
        l = lt_ref[...] * a_p + a_n
        o = (ot_ref[...] * _seg_expand(a_p, HEAD_DIM) + _seg_expand(a_n, HEAD_DIM) * vn) \
            / _seg_expand(l, HEAD_DIM)
        outs.append(o)
        lses.append(m + jnp.log(l))
    m = jnp.maximum(jnp.maximum(lses[0], lses[1]), lses[2])
    es = [jnp.exp(l - m) for l in lses]
    tot = es[0] + es[1] + es[2]
    o_t = sum(_seg_expand(e / tot, HEAD_DIM) * o for e, o in zip(es, outs))
    mix = _dot(o_t.T.astype(BF16), w_ref[...])
    out_ref[...] = _layer_norm(alpha * x_ref[...] + mix, g_ref[...], b_ref[...])


def _dil_merge(states, q_t, kvn_t, w, x, g, b, alpha):
    flat = [a for st in states for a in st]
    return pl.pallas_call(
        functools.partial(_dil_merge_kernel, alpha=alpha),
        out_shape=jax.ShapeDtypeStruct(x.shape, F32),
        compiler_params=pltpu.CompilerParams(vmem_limit_bytes=VMEM_LIMIT),
        name="dil_merge",
    )(*flat, q_t, kvn_t, w, x, g, b)


def kernel(x_prompt, x_sample, cache_fox_k, cache_fox_v, cache_fox_logf, cache_dil_kv_0, cache_dil_kv_1,
           cache_dil_kv_2, page_table, w_qkvf_a, b_f_a, w_o_a, w_q_b, w_kv_b, w_o_b, ln_g, ln_b,
           w_router_grp, b_router_grp, w_router_exp, b_router_exp, w_gate, w_up, w_down):
    nb, t, d = x_prompt.shape
    ns = x_sample.shape[0]
    depth = w_gate.shape[0]
    alpha = (2 * depth) ** 0.25
    nh = d // HEAD_DIM
    dg = w_o_b.shape[1]
    dq = N_DIL_GROUPS * dg

    w_a = w_qkvf_a[0]
    wk = w_a[:, d:2 * d].astype(BF16)
    wqkv_t = w_a[:, :3 * d].T.astype(BF16)
    wf = w_a[:, 3 * d:].astype(BF16)
    wf_t = wf.T
    b_f = b_f_a[0]
    wo_a = w_o_a[0].astype(BF16)
    wqkv_b_t = jnp.concatenate([w_q_b[0], w_kv_b], axis=1).T.astype(BF16)
    wkv_t = w_kv_b.T.astype(BF16)
    wo_b = w_o_b[0].astype(BF16)
    wqkv_g = [jnp.concatenate([w_q_b[0][:, g * dg:(g + 1) * dg] * ATTN_SCALE,
                               w_kv_b[:, 2 * g * dg:(2 * g + 2) * dg]], axis=1).astype(BF16)
              for g in range(N_DIL_GROUPS)]
    pad = LANES - N_EXPERT_GROUPS - N_EXPERTS
    wr = [jnp.pad(jnp.concatenate([w_router_grp[l], w_router_exp[l]], axis=1), ((0, 0), (0, pad)))
          for l in range(depth)]
    br = [jnp.pad(jnp.concatenate([b_router_grp[l], b_router_exp[l]]), (0, pad)).reshape(1, LANES)
          for l in range(depth)]
    wr_t = [jnp.concatenate([w_router_grp[l].T, w_router_exp[l].T, jnp.zeros((4, d), F32)], axis=0)
            for l in range(depth)]
    br_t = [jnp.concatenate([b_router_grp[l], b_router_exp[l], jnp.zeros((4,), F32)]).reshape(-1, 1)
            for l in range(depth)]
    lng = lambda l, i: ln_g[l, i].reshape(1, d)
    lnb = lambda l, i: ln_b[l, i].reshape(1, d)

    def moe(x, l):
        return _moe_norm(x, wr[l], br[l], l, w_gate, w_up, w_down, lng(l, 1), lnb(l, 1), alpha)

    def moe_routed(x, l):
        return _moe_routed_norm(x, wr_t[l], br_t[l], l, w_gate, w_up, w_down, lng(l, 1), lnb(l, 1), alpha)

    xp = x_prompt.reshape(nb * t, d)
    (q_t,) = _mm_t(wqkv_t[:d], x_prompt, (BF16,), scale=ATTN_SCALE * LOG2E)
    k_rm = _mm(xp, wk, BF16).reshape(nb, t, d)
    (k_t,) = _mm_t(wqkv_t[d:2 * d], x_prompt, (F32,))
    v_t, v_tb = _mm_t(wqkv_t[2 * d:], x_prompt, (F32, BF16))
    lf_t, cum_tok = _gate(x_prompt, wf_t, wf, b_f)
    o_t = _fox_attn(q_t, k_rm, v_tb, cum_tok)
    x1 = _proj_norm_t(o_t, wo_a, x_prompt, lng(0, 0), lnb(0, 0), alpha)
    x2 = moe_routed(x1.reshape(nb * t, d), 0).reshape(nb, t, d)

    tail = min(max(DIL_WINDOWS), t)
    (kv_tail_t,) = _mm_t(wkv_t, x2, (F32,), row_start=t - tail, rows=tail)
    os_, ls_ = zip(*[_dil_attn(_mm_dil(x2, wqkv_g[g], DIL_RATES[g]), g) for g in range(N_DIL_GROUPS)])
    x3 = _mix_norm(os_, ls_, wo_b, x2, lng(1, 0), lnb(1, 0), alpha)
    y_prompt = moe_routed(x3.reshape(nb * t, d), 1).reshape(nb, t, d)

    def feat_major_out(a_t, lead):
        b_, f_, t_ = a_t.shape
        return a_t.reshape(*lead, b_, f_ // HEAD_DIM, HEAD_DIM, t_).transpose(
            *range(len(lead)), len(lead), len(lead) + 3, len(lead) + 1, len(lead) + 2)

    fox_k_p = feat_major_out(k_t, (1,))
    fox_v_p = feat_major_out(v_t, (1,))
    fox_lf_p = lf_t.transpose(0, 2, 1)[None]
    dil_p = []
    for g, w in enumerate(DIL_WINDOWS):
        wlen = min(w, t)
        sl = kv_tail_t[:, 2 * g * dg:(2 * g + 2) * dg, tail - wlen:]
        dil_p.append(sl.reshape(nb, 2, dg // HEAD_DIM, HEAD_DIM, wlen).transpose(0, 4, 1, 2, 3))

    xs = x_sample.reshape(ns, d)
    (qkv_s,) = _mm_t(wqkv_t, xs[None], (F32,))
    q_st, kn_t, vn_t = qkv_s[0, :d], qkv_s[0, d:2 * d], qkv_s[0, 2 * d:]
    lfn_t, _ = _gate(xs[None], wf_t, wf, b_f)
    lfn_t = lfn_t[0]
    n_phys, ps = cache_fox_k.shape[1], cache_fox_k.shape[2]
    k_pages = cache_fox_k[0].transpose(0, 2, 3, 1).reshape(n_phys, d, ps)
    v_pages = cache_fox_v[0].transpose(0, 2, 3, 1).reshape(n_phys, d, ps)
    lf_pages = cache_fox_logf[0].transpose(0, 2, 1)
    o_t, m_t, l_t, c_t = _fox_decode(page_table, q_st, k_pages, v_pages, lf_pages)
    x1s = _fox_merge(o_t, m_t, l_t, c_t, q_st, kn_t, vn_t, lfn_t, wo_a, xs, lng(0, 0), lnb(0, 0), alpha)
    x2s = moe(x1s, 0)

    (qkv_b,) = _mm_t(wqkv_b_t, x2s[None], (F32,))
    qb_t, kvn_t = qkv_b[0, :dq], qkv_b[0, dq:]
    states, dil_s = [], []
    for g, cache in enumerate((cache_dil_kv_0, cache_dil_kv_1, cache_dil_kv_2)):
        wlen = cache.shape[1]
        buf_t = cache.transpose(0, 2, 3, 4, 1).reshape(ns, 2 * dg, wlen)
        new_buf, og, mg, lg = _dil_decode(buf_t, qb_t[g * dg:(g + 1) * dg], kvn_t[2 * g * dg:(2 * g + 2) * dg], g)
        states.append((og, mg, lg))
        dil_s.append(new_buf.reshape(ns, 2, dg // HEAD_DIM, HEAD_DIM, wlen).transpose(0, 4, 1, 2, 3))
    x3s = _dil_merge(states, qb_t, kvn_t, wo_b, x2s, lng(1, 0), lnb(1, 0), alpha)
    y_sample = moe(x3s, 1).reshape(ns, 1, d)

    fox_k_s = kn_t.reshape(1, 1, nh, HEAD_DIM, ns).transpose(0, 4, 1, 2, 3)
    fox_v_s = vn_t.reshape(1, 1, nh, HEAD_DIM, ns).transpose(0, 4, 1, 2, 3)
    fox_lf_s = lfn_t.reshape(1, 1, nh, ns).transpose(0, 3, 1, 2)

    return (y_prompt, y_sample, fox_k_p, fox_v_p, fox_lf_p, dil_p[0], dil_p[1], dil_p[2],
            fox_k_s, fox_v_s, fox_lf_s, dil_s[0], dil_s[1], dil_s[2])
```
